```python
import math
import jax
import jax.numpy as jnp
from jax import lax
import numpy as np

D_MODEL = 1024
BATCH = 8
SEQ = 2048
DEPTH = 2
DEC_BATCH = 128
DEC_SEQ = 1
PAST_LEN = 16384
PAGE_SIZE = 128

RWKV_HEADS = 8
RWKV_HEAD_DIM = 64
RWKV_WIDTH = RWKV_HEADS * RWKV_HEAD_DIM
DECAY_LORA = 64
ICLR_LORA = 64
GATE_LORA = 128
GN_EPS = 64e-5
MLA_HEADS = 8
QK_NOPE_DIM = 64
QK_ROPE_DIM = 32
V_HEAD_DIM = 64
MLA_WIDTH = MLA_HEADS * V_HEAD_DIM
Q_LORA = 256
KV_LORA = 128
ROPE_THETA = 10000.0
MLA_SCALE = (QK_NOPE_DIM + QK_ROPE_DIM) ** -0.5
Q_BLOCK = 128
RWKV_PROJ = 3 * RWKV_WIDTH + DECAY_LORA + ICLR_LORA + GATE_LORA
MLA_PROJ = Q_LORA + KV_LORA + QK_ROPE_DIM
IN_PROJ = RWKV_PROJ + MLA_PROJ
MIX_WIDTH = RWKV_WIDTH + MLA_WIDTH
N_MEM = 256
MEM_HEADS = 4
MEM_HEAD_DIM = D_MODEL // MEM_HEADS
D_FF = 2816
RMS_EPS = 1e-6

kernel_name = 'hymba_rwkv7_mla_macaron_step'


def rmsnorm(x, g):
    xf = x.astype(jnp.float32)
    y = xf * lax.rsqrt(jnp.mean(xf * xf, axis=-1, keepdims=True) + RMS_EPS)
    return (y * g.astype(jnp.float32)).astype(x.dtype)


def swiglu_ffn(h, w_gate, w_up, w_down):
    return (jax.nn.silu(h @ w_gate) * (h @ w_up)) @ w_down


def rope(x, pos):
    half = QK_ROPE_DIM // 2
    inv_freq = ROPE_THETA ** (-jnp.arange(half, dtype=jnp.float32) / half)
    ang = pos.astype(jnp.float32)[:, None] * inv_freq[None, :]
    ang = ang.reshape((ang.shape[0],) + (1,) * (x.ndim - 3) + (half,))
    cos, sin = jnp.cos(ang), jnp.sin(ang)
    xf = x.astype(jnp.float32)
    x1, x2 = xf[..., :half], xf[..., half:]
    return jnp.concatenate([x1 * cos - x2 * sin, x1 * sin + x2 * cos], axis=-1).astype(x.dtype)


def token_shift(p, prev, mu):
    p_prev = jnp.concatenate([prev[:, None, :].astype(p.dtype), p[:, :-1]], axis=1)
    return p + (p_prev - p) * mu, p[:, -1]


def rwkv7_mix(p, s0, w0, w_up, a0, a_up, g_up, k_k, k_a, r_k, gn_w, gn_b):
    B, T, _ = p.shape
    H, N, W = RWKV_HEADS, RWKV_HEAD_DIM, RWKV_WIDTH
    o1 = 3 * W
    r = p[..., :W]
    k = p[..., W:2 * W]
    v = p[..., 2 * W:o1]
    wd = p[..., o1:o1 + DECAY_LORA]
    ad = p[..., o1 + DECAY_LORA:o1 + DECAY_LORA + ICLR_LORA]
    gd = p[..., o1 + DECAY_LORA + ICLR_LORA:]
    w_log = -jax.nn.softplus(-(w0 + jnp.tanh(wd) @ w_up)) - 0.5
    decay = jnp.exp(-jnp.exp(w_log.astype(jnp.float32)))
    a = jax.nn.sigmoid(a0 + ad @ a_up)
    g = jax.nn.sigmoid(gd) @ g_up
    heads = lambda t: t.astype(jnp.float32).reshape(B, T, H, N)
    kk = heads(k * k_k)
    kk = kk / jnp.maximum(jnp.linalg.norm(kk, axis=-1, keepdims=True), 1e-12)
    k = k * (1.0 + (a - 1.0) * k_a)
    r_h, k_h, v_h, a_h, w_h = heads(r), heads(k), heads(v), heads(a), heads(decay)
    b_h = kk * a_h

    def step(S, inp):
        r_t, w_t, k_t, v_t, kk_t, b_t = inp
        S_kk = jnp.einsum('bhvk,bhk->bhv', S, kk_t)
        S = S * w_t[:, :, None, :] - S_kk[..., None] * b_t[:, :, None, :] + v_t[..., None] * k_t[:, :, None, :]
        return S, jnp.einsum('bhvk,bhk->bhv', S, r_t)

    seq_first = lambda t: jnp.swapaxes(t, 0, 1)
    xs = (seq_first(r_h), seq_first(w_h), seq_first(k_h), seq_first(v_h), seq_first(kk), seq_first(b_h))
    s_last, o = lax.scan(step, s0.astype(jnp.float32), xs)
    o = seq_first(o)
    mean = jnp.mean(o, axis=-1, keepdims=True)
    var = jnp.mean(jnp.square(o - mean), axis=-1, keepdims=True)
    o = ((o - mean) * lax.rsqrt(var + GN_EPS)).reshape(B, T, W) * gn_w + gn_b
    bonus = jnp.sum(r_h * k_h * r_k, axis=-1, keepdims=True) * v_h
    out = (o + bonus.reshape(B, T, W)) * g
    return out.astype(p.dtype), s_last.astype(s0.dtype)


def mla_project(pm, pos, q_norm, w_uq, kv_norm, w_uk):
    B, T, _ = pm.shape
    c_q = rmsnorm(pm[..., :Q_LORA], q_norm)
    c_kv = rmsnorm(pm[..., Q_LORA:Q_LORA + KV_LORA], kv_norm)
    k_rope = rope(pm[..., Q_LORA + KV_LORA:], pos)
    q = (c_q @ w_uq).reshape(B, T, MLA_HEADS, QK_NOPE_DIM + QK_ROPE_DIM)
    q_rope = rope(q[..., QK_NOPE_DIM:], pos)
    q_lat = jnp.einsum('bthd,chd->bthc', q[..., :QK_NOPE_DIM], w_uk)
    return q_lat, q_rope, c_kv, k_rope


def mla_scores(q_lat, q_rope, c_kv, k_rope):
    s = jnp.einsum('bqhc,bkc->bhqk', q_lat, c_kv) + jnp.einsum('bqhr,bkr->bhqk', q_rope, k_rope)
    return s.astype(jnp.float32) * MLA_SCALE


def mla_attend_prompt(q_lat, q_rope, c_kv, k_rope):
    B, T, H, C = q_lat.shape
    nb = T // Q_BLOCK
    blocks = lambda t: jnp.moveaxis(t.reshape((B, nb, Q_BLOCK) + t.shape[2:]), 1, 0)
    k_pos = jnp.arange(T)

    def one_block(args):
        i, ql, qr = args
        s = mla_scores(ql, qr, c_kv, k_rope)
        q_pos = i * Q_BLOCK + jnp.arange(Q_BLOCK)
        s = jnp.where(k_pos[None, :] <= q_pos[:, None], s, -jnp.inf)
        pr = jax.nn.softmax(s, axis=-1).astype(c_kv.dtype)
        return jnp.einsum('bhqk,bkc->bqhc', pr, c_kv)

    o = lax.map(one_block, (jnp.arange(nb), blocks(q_lat), blocks(q_rope)))
    return jnp.moveaxis(o, 0, 1).reshape(B, T, H, C)


def mla_attend_sample(q_lat, q_rope, c_new, kr_new, ckv_past, kr_past):
    T = q_lat.shape[1]
    P = ckv_past.shape[1]
    s_past = mla_scores(q_lat, q_rope, ckv_past, kr_past)
    causal = jnp.tril(jnp.ones((T, T), dtype=bool))
    s_new = jnp.where(causal, mla_scores(q_lat, q_rope, c_new, kr_new), -jnp.inf)
    pr = jax.nn.softmax(jnp.concatenate([s_past, s_new], axis=-1), axis=-1).astype(c_new.dtype)
    return (jnp.einsum('bhqk,bkc->bqhc', pr[..., :P], ckv_past)
            + jnp.einsum('bhqk,bkc->bqhc', pr[..., P:], c_new))


def memory_kv(mem, g, w_k, w_v):
    m = rmsnorm(mem, g)
    shp = (mem.shape[0], mem.shape[1], MEM_HEADS, MEM_HEAD_DIM)
    return (m @ w_k).reshape(shp), (m @ w_v).reshape(shp)


def memory_cross_attend(h, mem_k, mem_v, w_q, w_o):
    B, T, _ = h.shape
    q = (h @ w_q).reshape(B, T, MEM_HEADS, MEM_HEAD_DIM)
    s = jnp.einsum('bthd,bmhd->bhtm', q, mem_k).astype(jnp.float32) * (MEM_HEAD_DIM ** -0.5)
    pr = jax.nn.softmax(s, axis=-1).astype(mem_v.dtype)
    return jnp.einsum('bhtm,bmhd->bthd', pr, mem_v).reshape(B, T, D_MODEL) @ w_o


def setup_inputs(seed: int = 0) -> dict:
    key = jax.random.key(seed)
    ks = iter(jax.random.split(key, 64))

    def nrm(shape, scale=1.0):
        return jax.random.normal(next(ks), shape, jnp.float32) * scale

    def gain(shape):
        return 1.0 + nrm(shape, 0.02)

    L, D = DEPTH, D_MODEL
    n_pages = PAST_LEN // PAGE_SIZE
    n_phys = (DEC_BATCH * n_pages * 5) // 4
    page_table = jax.random.permutation(next(ks), n_phys)[:DEC_BATCH * n_pages]
    page_table = page_table.reshape(DEC_BATCH, n_pages).astype(jnp.int32)
    return {
        'x_prompt': nrm((BATCH, SEQ, D)),
        'x_sample': nrm((DEC_BATCH, DEC_SEQ, D)),
        'cache_ckv': nrm((L, n_phys, PAGE_SIZE, KV_LORA)),
        'cache_krope': nrm((L, n_phys, PAGE_SIZE, QK_ROPE_DIM)),
        'state_wkv': nrm((L, DEC_BATCH, RWKV_HEADS, RWKV_HEAD_DIM, RWKV_HEAD_DIM), 0.3),
        'state_shift': nrm((L, DEC_BATCH, RWKV_PROJ)),
        'cache_mem_k': nrm((L, DEC_BATCH, N_MEM, MEM_HEADS, MEM_HEAD_DIM)),
        'cache_mem_v': nrm((L, DEC_BATCH, N_MEM, MEM_HEADS, MEM_HEAD_DIM)),
        'page_table': page_table,
        'mem_prompt': nrm((BATCH, N_MEM, D)),
        'norm_ffn1': gain((L, D)),
        'ffn1_w_gate': nrm((L, D, D_FF), D ** -0.5),
        'ffn1_w_up': nrm((L, D, D_FF), D ** -0.5),
        'ffn1_w_down': nrm((L, D_FF, D), D_FF ** -0.5),
        'norm_mix': gain((L, D)),
        'w_in': nrm((L, D, IN_PROJ), D ** -0.5),
        'shift_mu': jax.random.uniform(next(ks), (L, RWKV_PROJ), jnp.float32, 0.1, 0.9),
        'rwkv_w0': nrm((L, RWKV_WIDTH), 0.5),
        'rwkv_w_up': nrm((L, DECAY_LORA, RWKV_WIDTH), 0.5 * DECAY_LORA ** -0.5),
        'rwkv_a0': nrm((L, RWKV_WIDTH), 0.5),
        'rwkv_a_up': nrm((L, ICLR_LORA, RWKV_WIDTH), 0.5 * ICLR_LORA ** -0.5),
        'rwkv_g_up': nrm((L, GATE_LORA, RWKV_WIDTH), 2.0 * GATE_LORA ** -0.5),
        'rwkv_k_k': 0.85 + nrm((L, RWKV_WIDTH), 0.05),
        'rwkv_k_a': 1.0 + nrm((L, RWKV_WIDTH), 0.05),
        'rwkv_r_k': nrm((L, RWKV_HEADS, RWKV_HEAD_DIM), 0.1),
        'rwkv_gn_w': gain((L, RWKV_WIDTH)),
        'rwkv_gn_b': nrm((L, RWKV_WIDTH), 0.02),
        'mla_q_norm': gain((L, Q_LORA)),
        'mla_w_uq': nrm((L, Q_LORA, MLA_HEADS * (QK_NOPE_DIM + QK_ROPE_DIM)), Q_LORA ** -0.5),
        'mla_kv_norm': gain((L, KV_LORA)),
        'mla_w_uk': nrm((L, KV_LORA, MLA_HEADS, QK_NOPE_DIM), KV_LORA ** -0.5),
        'mla_w_uv': nrm((L, KV_LORA, MLA_HEADS, V_HEAD_DIM), KV_LORA ** -0.5),
        'w_out': nrm((L, MIX_WIDTH, D), MIX_WIDTH ** -0.5),
        'norm_cross': gain((L, D)),
        'norm_mem': gain((L, D)),
        'mem_w_q': nrm((L, D, D), D ** -0.5),
        'mem_w_k': nrm((L, D, D), D ** -0.5),
        'mem_w_v': nrm((L, D, D), D ** -0.5),
        'mem_w_o': nrm((L, D, D), D ** -0.5),
        'norm_ffn2': gain((L, D)),
        'ffn2_w_gate': nrm((L, D, D_FF), D ** -0.5),
        'ffn2_w_up': nrm((L, D, D_FF), D ** -0.5),
        'ffn2_w_down': nrm((L, D_FF, D), D_FF ** -0.5),
        'norm_final': gain((D,)),
    }


def reference(x_prompt, x_sample, cache_ckv, cache_krope, state_wkv, state_shift, cache_mem_k, cache_mem_v,
              page_table, mem_prompt,
              norm_ffn1, ffn1_w_gate, ffn1_w_up, ffn1_w_down, norm_mix, w_in, shift_mu,
              rwkv_w0, rwkv_w_up, rwkv_a0, rwkv_a_up, rwkv_g_up, rwkv_k_k, rwkv_k_a, rwkv_r_k, rwkv_gn_w, rwkv_gn_b,
              mla_q_norm, mla_w_uq, mla_kv_norm, mla_w_uk, mla_w_uv, w_out,
              norm_cross, norm_mem, mem_w_q, mem_w_k, mem_w_v, mem_w_o,
              norm_ffn2, ffn2_w_gate, ffn2_w_up, ffn2_w_down, norm_final):

    def trunk(x, pos, shift0, wkv0, mem_k, mem_v, attend):
        B, T, _ = x.shape
        ckvs, krs, wkvs, shifts = [], [], [], []
        for l in range(DEPTH):
            h = rmsnorm(x, norm_ffn1[l])
            x = x + 0.5 * swiglu_ffn(h, ffn1_w_gate[l], ffn1_w_up[l], ffn1_w_down[l])
            h = rmsnorm(x, norm_mix[l])
            proj = h @ w_in[l]
            p_rwkv, shift_last = token_shift(proj[..., :RWKV_PROJ], shift0[l], shift_mu[l])
            o_rwkv, wkv_last = rwkv7_mix(p_rwkv, wkv0[l], rwkv_w0[l], rwkv_w_up[l], rwkv_a0[l], rwkv_a_up[l],
                                         rwkv_g_up[l], rwkv_k_k[l], rwkv_k_a[l], rwkv_r_k[l],
                                         rwkv_gn_w[l], rwkv_gn_b[l])
            q_lat, q_rope, c_kv, k_rope = mla_project(proj[..., RWKV_PROJ:], pos, mla_q_norm[l], mla_w_uq[l],
                                                      mla_kv_norm[l], mla_w_uk[l])
            o_lat = attend(l, q_lat, q_rope, c_kv, k_rope)
            o_mla = jnp.einsum('bthc,chd->bthd', o_lat, mla_w_uv[l]).reshape(B, T, MLA_WIDTH)
            x = x + jnp.concatenate([o_rwkv, o_mla], axis=-1) @ w_out[l]
            h = rmsnorm(x, norm_cross[l])
            x = x + memory_cross_attend(h, mem_k[l], mem_v[l], mem_w_q[l], mem_w_o[l])
            h = rmsnorm(x, norm_ffn2[l])
            x = x + 0.5 * swiglu_ffn(h, ffn2_w_gate[l], ffn2_w_up[l], ffn2_w_down[l])
            ckvs.append(c_kv)
            krs.append(k_rope)
            wkvs.append(wkv_last)
            shifts.append(shift_last)
        return rmsnorm(x, norm_final), jnp.stack(ckvs), jnp.stack(krs), jnp.stack(wkvs), jnp.stack(shifts)

    Bp, Tp, _ = x_prompt.shape
    mem_k_list, mem_v_list = [], []
    for l in range(DEPTH):
        mk, mv = memory_kv(mem_prompt, norm_mem[l], mem_w_k[l], mem_w_v[l])
        mem_k_list.append(mk)
        mem_v_list.append(mv)
    new_mem_k_prompt = jnp.stack(mem_k_list)
    new_mem_v_prompt = jnp.stack(mem_v_list)
    y_prompt, new_ckv_prompt, new_krope_prompt, new_wkv_prompt, new_shift_prompt = trunk(
        x_prompt, jnp.arange(Tp),
        jnp.zeros((DEPTH, Bp, RWKV_PROJ), x_prompt.dtype),
        jnp.zeros((DEPTH, Bp, RWKV_HEADS, RWKV_HEAD_DIM, RWKV_HEAD_DIM), state_wkv.dtype),
        new_mem_k_prompt, new_mem_v_prompt,
        lambda l, ql, qr, c, k: mla_attend_prompt(ql, qr, c, k))

    Bd, Td, _ = x_sample.shape
    past_len = page_table.shape[1] * PAGE_SIZE

    def attend_sample(l, ql, qr, c, k):
        ckv_past = cache_ckv[l, page_table].reshape(Bd, past_len, KV_LORA)
        kr_past = cache_krope[l, page_table].reshape(Bd, past_len, QK_ROPE_DIM)
        return mla_attend_sample(ql, qr, c, k, ckv_past, kr_past)

    y_sample, new_ckv_sample, new_krope_sample, new_wkv_sample, new_shift_sample = trunk(
        x_sample, past_len + jnp.arange(Td), state_shift, state_wkv, cache_mem_k, cache_mem_v, attend_sample)

    return (y_prompt, y_sample, new_ckv_prompt, new_krope_prompt, new_wkv_prompt, new_shift_prompt,
            new_mem_k_prompt, new_mem_v_prompt, new_ckv_sample, new_krope_sample, new_wkv_sample, new_shift_sample)
```

```python
import functools

import jax
import jax.numpy as jnp
from jax import lax
from jax.experimental import pallas as pl
from jax.experimental.pallas import tpu as pltpu

F32 = jnp.float32
BF16 = jnp.bfloat16
HI = lax.Precision.HIGHEST

D_MODEL = 1024
DEPTH = 2
PAGE_SIZE = 128
RWKV_HEADS = 8
RWKV_HEAD_DIM = 64
RWKV_WIDTH = 512
DECAY_LORA = 64
ICLR_LORA = 64
GATE_LORA = 128
GN_EPS = 64e-5
MLA_HEADS = 8
QK_NOPE_DIM = 64
QK_ROPE_DIM = 32
V_HEAD_DIM = 64
Q_LORA = 256
KV_LORA = 128
ROPE_THETA = 10000.0
MLA_SCALE = (QK_NOPE_DIM + QK_ROPE_DIM) ** -0.5
RWKV_PROJ = 3 * RWKV_WIDTH + DECAY_LORA + ICLR_LORA + GATE_LORA
N_MEM = 256
MEM_HEADS = 4
MEM_HEAD_DIM = D_MODEL // MEM_HEADS
D_FF = 2816
RMS_EPS = 1e-6

LANES = 128
IN_PROJ_EXT = RWKV_PROJ + Q_LORA + KV_LORA + 2 * LANES
LORA_SLAB = DECAY_LORA + ICLR_LORA + GATE_LORA
QCAT = 2 * LANES
CHUNK = 64
PAGES_PER_STEP = 16
NEG = -1e30
VMEM_LIMIT = 56 * 1024 * 1024


def _cp(sem):
    return pltpu.CompilerParams(dimension_semantics=sem, vmem_limit_bytes=VMEM_LIMIT)


def _dot(a, b, prec=None):
    return jnp.dot(a, b, preferred_element_type=F32, precision=prec)


def _dot_nt(a, b, prec=None):
    return lax.dot_general(a, b, (((1,), (1,)), ((), ())), preferred_element_type=F32, precision=prec)


def _rms(x, g):
    return x * lax.rsqrt(jnp.mean(x * x, axis=-1, keepdims=True) + RMS_EPS) * g


def _sigmoid(x):
    return 1.0 / (1.0 + jnp.exp(-x))


def _norm_matmul_kernel(x_ref, g_ref, w_ref, o_ref):
    h = _rms(x_ref[...], g_ref[...]).astype(BF16)
    o_ref[...] = _dot(h, w_ref[...]).astype(o_ref.dtype)


def norm_matmul(x, g, w, out_dtype, tm):
    m, d = x.shape
    n = w.shape[1]
    return pl.pallas_call(
        _norm_matmul_kernel,
        grid=(m // tm,),
        in_specs=[pl.BlockSpec((tm, d), lambda i: (i, 0)),
                  pl.BlockSpec((1, d), lambda i: (0, 0)),
                  pl.BlockSpec((d, n), lambda i: (0, 0))],
        out_specs=pl.BlockSpec((tm, n), lambda i: (i, 0)),
        out_shape=jax.ShapeDtypeStruct((m, n), out_dtype),
        compiler_params=_cp(("parallel",)),
        name="norm_matmul",
    )(x, g, w)


def _ffn_kernel(x_ref, g_ref, wg_ref, wu_ref, wd_ref, gf_ref, o_ref, h_scr, acc_scr, *, nf, final_norm):
    f = pl.program_id(1)

    @pl.when(f == 0)
    def _():
        h_scr[...] = _rms(x_ref[...], g_ref[...]).astype(BF16)
        acc_scr[...] = jnp.zeros_like(acc_scr)

    h = h_scr[...]
    gate = _dot(h, wg_ref[...])
    up = _dot(h, wu_ref[...])
    act = (gate * _sigmoid(gate) * up).astype(BF16)
    acc_scr[...] += _dot(act, wd_ref[...])

    @pl.when(f == nf - 1)
    def _():
        y = x_ref[...] + 0.5 * acc_scr[...]
        if final_norm:
            y = _rms(y, gf_ref[...])
        o_ref[...] = y


def ffn(x, g, wg, wu, wd, gf, final_norm, tm, tf):
    m, d = x.shape
    dff = wg.shape[1]
    nf = dff // tf
    return pl.pallas_call(
        functools.partial(_ffn_kernel, nf=nf, final_norm=final_norm),
        grid=(m // tm, nf),
        in_specs=[pl.BlockSpec((tm, d), lambda i, f: (i, 0)),
                  pl.BlockSpec((1, d), lambda i, f: (0, 0)),
                  pl.BlockSpec((d, tf), lambda i, f: (0, f)),
                  pl.BlockSpec((d, tf), lambda i, f: (0, f)),
                  pl.BlockSpec((tf, d), lambda i, f: (f, 0)),
                  pl.BlockSpec((1, d), lambda i, f: (0, 0))],
        out_specs=pl.BlockSpec((tm, d), lambda i, f: (i, 0)),
        out_shape=jax.ShapeDtypeStruct((m, d), F32),
        scratch_shapes=[pltpu.VMEM((tm, d), BF16), pltpu.VMEM((tm, d), F32)],
        compiler_params=_cp(("parallel", "arbitrary")),
        name="ffn",
    )(x, g, wg, wu, wd, gf)


def _matmul_res_kernel(a_ref, w_ref, x_ref, o_ref):
    o_ref[...] = x_ref[...] + _dot(a_ref[...], w_ref[...])


def matmul_res(a, w, x, tm):
    m, k = a.shape
    n = w.shape[1]
    return pl.pallas_call(
        _matmul_res_kernel,
        grid=(m // tm,),
        in_specs=[pl.BlockSpec((tm, k), lambda i: (i, 0)),
                  pl.BlockSpec((k, n), lambda i: (0, 0)),
                  pl.BlockSpec((tm, n), lambda i: (i, 0))],
        out_specs=pl.BlockSpec((tm, n), lambda i: (i, 0)),
        out_shape=jax.ShapeDtypeStruct((m, n), F32),
        compiler_params=_cp(("parallel",)),
        name="matmul_res",
    )(a, w, x)


def _mix_out_kernel(orw_ref, olat_ref, wuv_ref, wout_ref, x_ref, o_ref):
    o_mla = _dot(olat_ref[...], wuv_ref[...]).astype(BF16)
    y = _dot(orw_ref[...], wout_ref[:RWKV_WIDTH, :]) + _dot(o_mla, wout_ref[RWKV_WIDTH:, :])
    o_ref[...] = x_ref[...] + y


def mix_out(o_rwkv, o_lat, wuv_bd, w_out, x, tm):
    m, d = x.shape
    return pl.pallas_call(
        _mix_out_kernel,
        grid=(m // tm,),
        in_specs=[pl.BlockSpec((tm, RWKV_WIDTH), lambda i: (i, 0)),
                  pl.BlockSpec((tm, MLA_HEADS * KV_LORA), lambda i: (i, 0)),
                  pl.BlockSpec(wuv_bd.shape, lambda i: (0, 0)),
                  pl.BlockSpec(w_out.shape, lambda i: (0, 0)),
                  pl.BlockSpec((tm, d), lambda i: (i, 0))],
        out_specs=pl.BlockSpec((tm, d), lambda i: (i, 0)),
        out_shape=jax.ShapeDtypeStruct((m, d), F32),
        compiler_params=_cp(("parallel",)),
        name="mix_out",
    )(o_rwkv, o_lat, wuv_bd, w_out, x)


def _prep_kernel(proj_ref, prev_ref, mu_ref, w0_ref, a0_ref, wl_ref, kk_ref, ka_ref, ones_ref,
                 qn_ref, kvn_ref, wq_ref, wuk_ref, cos_ref, sin_ref,
                 r_o, k_o, v_o, kkn_o, b_o, lw_o, g_o, ckv_o, kr_o, qcat_o, kcat_o, last_o,
                 carry, *, tm, explicit_prev):
    p = proj_ref[0, :, :RWKV_PROJ]
    if explicit_prev:
        prev = prev_ref[0]
    else:
        @pl.when(pl.program_id(1) == 0)
        def _():
            carry[...] = prev_ref[0]

        row = lax.broadcasted_iota(jnp.int32, (tm, 1), 0)
        prev = jnp.where(row == 0, carry[...], pltpu.roll(p, 1, axis=0))
        carry[...] = p[tm - 1:tm, :]
    last_o[0] = p[tm - 1:tm, :]
    ps = p + (prev - p) * mu_ref[...]

    w3 = RWKV_WIDTH
    r = ps[:, :w3]
    k = ps[:, w3:2 * w3]
    v = ps[:, 2 * w3:3 * w3]
    slab = ps[:, 3 * w3:]
    lane = lax.broadcasted_iota(jnp.int32, slab.shape, 1)
    act = jnp.where(lane < DECAY_LORA, jnp.tanh(slab),
                    jnp.where(lane < DECAY_LORA + ICLR_LORA, slab, _sigmoid(slab)))
    lo = _dot(act, wl_ref[...], HI)
    z = -(w0_ref[...] + lo[:, :w3])
    softplus = jnp.maximum(z, 0.0) + jnp.log(1.0 + jnp.exp(-jnp.abs(z)))
    lw = -jnp.exp(-softplus - 0.5)
    a = _sigmoid(a0_ref[...] + lo[:, w3:2 * w3])
    g = lo[:, 2 * w3:]
    kkr = k * kk_ref[...]
    ssq = _dot(kkr * kkr, ones_ref[...], HI)
    kkn = kkr / jnp.maximum(jnp.sqrt(ssq), 1e-12)
    r_o[0] = r
    k_o[0] = k * (1.0 + (a - 1.0) * ka_ref[...])
    v_o[0] = v
    kkn_o[0] = kkn
    b_o[0] = kkn * a
    lw_o[0] = lw
    g_o[0] = g

    o1 = RWKV_PROJ
    cq = _rms(proj_ref[0, :, o1:o1 + Q_LORA], qn_ref[...]).astype(BF16)
    o2 = o1 + Q_LORA
    ckv = _rms(proj_ref[0, :, o2:o2 + KV_LORA], kvn_ref[...])
    o3 = o2 + KV_LORA
    cos = cos_ref[...]
    sin = sin_ref[...]
    kr = proj_ref[0, :, o3:o3 + LANES] * cos + proj_ref[0, :, o3 + LANES:o3 + 2 * LANES] * sin
    ckv_o[0] = ckv
    kr_o[0] = kr[:, :QK_ROPE_DIM]
    kcat_o[0] = jnp.concatenate([ckv, kr], axis=1).astype(BF16)

    qq = _dot(cq, wq_ref[...])
    nn = MLA_HEADS * QK_NOPE_DIM
    nr = MLA_HEADS * LANES
    q_lat = _dot(qq[:, :nn].astype(BF16), wuk_ref[...]) * MLA_SCALE
    cos8 = jnp.concatenate([cos] * MLA_HEADS, axis=1)
    sin8 = jnp.concatenate([sin] * MLA_HEADS, axis=1)
    q_rp = (qq[:, nn:nn + nr] * cos8 + qq[:, nn + nr:] * sin8) * MLA_SCALE
    pieces = []
    for h in range(MLA_HEADS):
        pieces.append(q_lat[:, h * LANES:(h + 1) * LANES])
        pieces.append(q_rp[:, h * LANES:(h + 1) * LANES])
    qcat_o[0] = jnp.concatenate(pieces, axis=1).astype(BF16)


def prep(proj, prev, explicit_prev, mu, w0, a0, w_lora, k_k, k_a, ones_bd, q_norm, kv_norm, w_q, wuk_bd, cos_t, sin_t, tm):
    b, t, n = proj.shape
    nt = t // tm
    wide = lambda w: pl.BlockSpec((1, tm, w), lambda i, j: (i, j, 0))
    full = lambda arr: pl.BlockSpec(arr.shape, lambda i, j: (0,) * arr.ndim)
    if explicit_prev:
        prev_spec = pl.BlockSpec((1, tm, RWKV_PROJ), lambda i, j: (i, j, 0))
    else:
        prev_spec = pl.BlockSpec((1, 1, RWKV_PROJ), lambda i, j: (i, 0, 0))
    sd = lambda w, dt: jax.ShapeDtypeStruct((b, t, w), dt)
    w3 = RWKV_WIDTH
    out_shape = [sd(w3, F32)] * 7 + [sd(KV_LORA, F32), sd(QK_ROPE_DIM, F32),
                                      sd(MLA_HEADS * QCAT, BF16), sd(QCAT, BF16),
                                      jax.ShapeDtypeStruct((b, 1, RWKV_PROJ), F32)]
    out_specs = [wide(w3)] * 7 + [wide(KV_LORA), wide(QK_ROPE_DIM), wide(MLA_HEADS * QCAT), wide(QCAT),
                                   pl.BlockSpec((1, 1, RWKV_PROJ), lambda i, j: (i, 0, 0))]
    return pl.pallas_call(
        functools.partial(_prep_kernel, tm=tm, explicit_prev=explicit_prev),
        grid=(b, nt),
        in_specs=[wide(n), prev_spec, full(mu), full(w0), full(a0), full(w_lora), full(k_k), full(k_a),
                  full(ones_bd), full(q_norm), full(kv_norm), full(w_q), full(wuk_bd),
                  pl.BlockSpec((tm, LANES), lambda i, j: (j, 0)),
                  pl.BlockSpec((tm, LANES), lambda i, j: (j, 0))],
        out_specs=out_specs,
        out_shape=out_shape,
        scratch_shapes=[pltpu.VMEM((1, RWKV_PROJ), F32)],
        compiler_params=_cp(("arbitrary", "arbitrary")),
        name="prep",
    )(proj, prev, mu, w0, a0, w_lora, k_k, k_a, ones_bd, q_norm, kv_norm, w_q, wuk_bd, cos_t, sin_t)


def _rwkv_post(o, r, k, v, g, rk, gnw, gnb, ones_bd):
    inv_n = 1.0 / RWKV_HEAD_DIM
    mean = _dot(o, ones_bd, HI) * inv_n
    d = o - mean
    var = _dot(d * d, ones_bd, HI) * inv_n
    on = d * lax.rsqrt(var + GN_EPS) * gnw + gnb
    bonus = _dot(r * k * rk, ones_bd, HI) * v
    return (on + bonus) * g


def _rwkv_chunk_kernel(r_ref, k_ref, v_ref, kk_ref, b_ref, lw_ref, g_ref, rk_ref, gnw_ref, gnb_ref,
                       o_ref, s_out_ref, s_scr, *, c, n_chunks):
    @pl.when(pl.program_id(1) == 0)
    def _():
        s_scr[...] = jnp.zeros_like(s_scr)

    n = RWKV_HEAD_DIM
    row = lax.broadcasted_iota(jnp.int32, (c, c), 0)
    col = lax.broadcasted_iota(jnp.int32, (c, c), 1)
    incl = row >= col
    strict = row > col
    tri = jnp.where(incl, 1.0, 0.0).astype(F32)
    eye = jnp.where(row == col, 1.0, 0.0).astype(F32)
    lane = lax.broadcasted_iota(jnp.int32, (c, LANES), 1)
    is_e = lane < n
    r2 = lax.broadcasted_iota(jnp.int32, (LANES, LANES), 0)
    c2 = lax.broadcasted_iota(jnp.int32, (LANES, LANES), 1)
    same_head = (r2 < n) == (c2 < n)
    ones_bd = jnp.where(same_head, 1.0, 0.0).astype(F32)
    n_double = c.bit_length() - 2

    for j in range(RWKV_HEADS // 2):
        sl = slice(j * LANES, (j + 1) * LANES)
        r = r_ref[0, :, sl]
        k = k_ref[0, :, sl]
        v = v_ref[0, :, sl]
        kk = kk_ref[0, :, sl]
        bv = b_ref[0, :, sl]
        lw = lw_ref[0, :, sl]
        cum = _dot(tri, lw, HI)
        e_pos = jnp.exp(cum)
        e_neg = jnp.exp(-cum)
        a_t = -kk * jnp.exp(cum - lw)
        r_t = r * e_pos
        b_t = bv * e_neg
        k_t = k * e_neg
        g_c = e_pos[c - 1:c, :]
        zero = jnp.zeros_like(a_t)
        lhs = jnp.concatenate([jnp.where(is_e, a_t, zero), jnp.where(is_e, r_t, zero),
                               jnp.where(is_e, zero, a_t), jnp.where(is_e, zero, r_t)], axis=0)
        xb = _dot_nt(lhs, b_t, HI)
        xk = _dot_nt(lhs, k_t, HI)
        w1 = []
        w2 = []
        o2 = []
        arb = []
        for x in range(2):
            base = 2 * c * x
            nmat = jnp.where(strict, xb[base:base + c], 0.0)
            aak = jnp.where(strict, xk[base:base + c], 0.0)
            arb.append(jnp.where(incl, xb[base + c:base + 2 * c], 0.0))
            ark = jnp.where(incl, xk[base + c:base + 2 * c], 0.0)
            tinv = eye + nmat
            pw = nmat
            for _ in range(n_double):
                pw = _dot(pw, pw, HI)
                tinv = tinv + _dot(pw, tinv, HI)
            w1.append(_dot(tinv, a_t, HI))
            w2.append(_dot(tinv, _dot(aak, v, HI), HI))
            o2.append(_dot(ark, v, HI))
        s0 = s_scr[j]
        u = _dot_nt(jnp.where(is_e, w1[0], w1[1]), s0, HI) + jnp.where(is_e, w2[0], w2[1])
        o = (_dot_nt(r_t, s0, HI) + jnp.where(is_e, _dot(arb[0], u, HI), _dot(arb[1], u, HI))
             + jnp.where(is_e, o2[0], o2[1]))
        uv = jnp.concatenate([u, v], axis=0)
        bk = jnp.concatenate([b_t * g_c, k_t * g_c], axis=0)
        s_new = s0 * g_c + jnp.where(same_head, _dot(uv.T, bk, HI), 0.0)
        s_scr[j] = s_new
        o_ref[0, :, sl] = _rwkv_post(o, r, k, v, g_ref[0, :, sl], rk_ref[:, sl], gnw_ref[:, sl], gnb_ref[:, sl],
                                     ones_bd).astype(o_ref.dtype)

    @pl.when(pl.program_id(1) == n_chunks - 1)
    def _():
        s_out_ref[0] = s_scr[...]


def rwkv_chunked(r, k, v, kk, bv, lw, g, rk, gnw, gnb, c):
    b, t, w = r.shape
    n_chunks = t // c
    seq = pl.BlockSpec((1, c, w), lambda i, j: (i, j, 0))
    par = pl.BlockSpec((1, w), lambda i, j: (0, 0))
    npair = RWKV_HEADS // 2
    return pl.pallas_call(
        functools.partial(_rwkv_chunk_kernel, c=c, n_chunks=n_chunks),
        grid=(b, n_chunks),
        in_specs=[seq] * 7 + [par] * 3,
        out_specs=[seq, pl.BlockSpec((1, npair, LANES, LANES), lambda i, j: (i, 0, 0, 0))],
        out_shape=[jax.ShapeDtypeStruct((b, t, w), BF16),
                   jax.ShapeDtypeStruct((b, npair, LANES, LANES), F32)],
        scratch_shapes=[pltpu.VMEM((npair, LANES, LANES), F32)],
        compiler_params=_cp(("arbitrary", "arbitrary")),
        name="rwkv_chunked",
    )(r, k, v, kk, bv, lw, g, rk, gnw, gnb)


def _rwkv_step_kernel(r_ref, k_ref, v_ref, kk_ref, b_ref, lw_ref, g_ref, rk_ref, gnw_ref, gnb_ref, s_ref,
                      o_ref, s_out_ref, *, bb):
    n = RWKV_HEAD_DIM
    eye = lax.broadcasted_iota(jnp.int32, (n, n), 0) == lax.broadcasted_iota(jnp.int32, (n, n), 1)

    def body(i, carry):
        for h in range(RWKV_HEADS):
            hs = slice(h, h + 1)
            s = s_ref[i, h]
            r = r_ref[i, hs, :]
            k = k_ref[i, hs, :]
            v = v_ref[i, hs, :]
            kk = kk_ref[i, hs, :]
            bv = b_ref[i, hs, :]
            w = jnp.exp(lw_ref[i, hs, :])
            s_kk = jnp.sum(s * kk, axis=1, keepdims=True)
            v_col = jnp.sum(jnp.where(eye, v, 0.0), axis=1, keepdims=True)
            s_new = s * w - s_kk * bv + v_col * k
            o_col = jnp.sum(s_new * r, axis=1, keepdims=True)
            o = jnp.sum(jnp.where(eye, o_col, 0.0), axis=0, keepdims=True)
            mean = jnp.mean(o, axis=1, keepdims=True)
            d = o - mean
            var = jnp.mean(d * d, axis=1, keepdims=True)
            on = d * lax.rsqrt(var + GN_EPS) * gnw_ref[hs, :] + gnb_ref[hs, :]
            bonus = jnp.sum(r * k * rk_ref[hs, :], axis=1, keepdims=True) * v
            o_ref[i, hs, :] = ((on + bonus) * g_ref[i, hs, :]).astype(o_ref.dtype)
            s_out_ref[i, h] = s_new
        return carry

    lax.fori_loop(0, bb, body, 0)


def rwkv_step(r, k, v, kk, bv, lw, g, rk, gnw, gnb, s, bb):
    b = r.shape[0]
    h, n = RWKV_HEADS, RWKV_HEAD_DIM
    vec = pl.BlockSpec((bb, h, n), lambda i: (i, 0, 0))
    par = pl.BlockSpec((h, n), lambda i: (0, 0))
    st = pl.BlockSpec((bb, h, n, n), lambda i: (i, 0, 0, 0))
    return pl.pallas_call(
        functools.partial(_rwkv_step_kernel, bb=bb),
        grid=(b // bb,),
        in_specs=[vec] * 7 + [par] * 3 + [st],
        out_specs=[vec, st],
        out_shape=[jax.ShapeDtypeStruct((b, h, n), F32), jax.ShapeDtypeStruct((b, h, n, n), F32)],
        compiler_params=_cp(("parallel",)),
        name="rwkv_step",
    )(r, k, v, kk, bv, lw, g, rk, gnw, gnb, s)


def _mla_prompt_kernel(q_ref, k_ref, o_ref, *, tq, tk):
    i = pl.program_id(1)
    q_pos = i * tq + lax.broadcasted_iota(jnp.int32, (tq, 1), 0)
    k_iota = lax.broadcasted_iota(jnp.int32, (1, tk), 1)
    n_kv = (i * tq + tq + tk - 1) // tk

    for h in range(MLA_HEADS):
        q = q_ref[0, :, h * QCAT:(h + 1) * QCAT]

        def body(j, carry):
            m, l, acc = carry
            kb = k_ref[0, pl.ds(pl.multiple_of(j * tk, tk), tk), :]
            s = _dot_nt(q, kb)
            s = jnp.where(j * tk + k_iota <= q_pos, s, NEG)
            m_new = jnp.maximum(m, jnp.max(s, axis=1, keepdims=True))
            p = jnp.exp(s - m_new)
            alpha = jnp.exp(m - m_new)
            l = alpha * l + jnp.sum(p, axis=1, keepdims=True)
            acc = alpha * acc + _dot(p.astype(BF16), kb[:, :KV_LORA])
            return m_new, l, acc

        init = (jnp.full((tq, 1), NEG, F32), jnp.zeros((tq, 1), F32), jnp.zeros((tq, KV_LORA), F32))
        m, l, acc = lax.fori_loop(0, n_kv, body, init)
        o_ref[0, :, h * KV_LORA:(h + 1) * KV_LORA] = (acc / l).astype(o_ref.dtype)


def mla_prompt(q_cat, k_cat, tq, tk):
    b, t, _ = q_cat.shape
    return pl.pallas_call(
        functools.partial(_mla_prompt_kernel, tq=tq, tk=tk),
        grid=(b, t // tq),
        in_specs=[pl.BlockSpec((1, tq, MLA_HEADS * QCAT), lambda i, j: (i, j, 0)),
                  pl.BlockSpec((1, t, QCAT), lambda i, j: (i, 0, 0))],
        out_specs=pl.BlockSpec((1, tq, MLA_HEADS * KV_LORA), lambda i, j: (i, j, 0)),
        out_shape=jax.ShapeDtypeStruct((b, t, MLA_HEADS * KV_LORA), BF16),
        compiler_params=_cp(("parallel", "arbitrary")),
        name="mla_prompt",
    )(q_cat, k_cat)


def _mla_sample_kernel(pt_ref, q_ref, knew_ref, *refs, n_steps):
    del pt_ref
    npg = PAGES_PER_STEP
    ckv_refs = refs[:npg]
    kr_refs = refs[npg:2 * npg]
    o_ref, m_scr, l_scr, acc_scr = refs[2 * npg:]
    j = pl.program_id(1)

    @pl.when(j == 0)
    def _():
        m_scr[...] = jnp.full_like(m_scr, NEG)
        l_scr[...] = jnp.zeros_like(l_scr)
        acc_scr[...] = jnp.zeros_like(acc_scr)

    q = q_ref[0]
    ql = q[:, :KV_LORA]
    qr = q[:, KV_LORA:KV_LORA + QK_ROPE_DIM]
    pages = [ckv_refs[i][0, 0].astype(BF16) for i in range(npg)]
    s = jnp.concatenate([_dot_nt(ql, pages[i]) + _dot_nt(qr, kr_refs[i][0, 0].astype(BF16))
                         for i in range(npg)], axis=1)
    m = m_scr[...]
    m_new = jnp.maximum(m, jnp.max(s, axis=1, keepdims=True))
    p = jnp.exp(s - m_new)
    alpha = jnp.exp(m - m_new)
    l = alpha * l_scr[...] + jnp.sum(p, axis=1, keepdims=True)
    acc = alpha * acc_scr[...]
    for i in range(npg):
        acc = acc + _dot(p[:, i * PAGE_SIZE:(i + 1) * PAGE_SIZE].astype(BF16), pages[i])
    m_scr[...] = m_new
    l_scr[...] = l
    acc_scr[...] = acc

    @pl.when(j == n_steps - 1)
    def _():
        k_new = knew_ref[0].astype(F32)
        s_new = jnp.sum(q.astype(F32) * k_new, axis=1, keepdims=True)
        m_fin = jnp.maximum(m_new, s_new)
        a2 = jnp.exp(m_new - m_fin)
        p_new = jnp.exp(s_new - m_fin)
        l_fin = a2 * l + p_new
        acc_fin = a2 * acc + p_new * k_new[:, :KV_LORA]
        o_ref[0] = (acc_fin / l_fin).astype(o_ref.dtype)


def mla_sample(page_table_flat, q, k_new, cache_ckv, cache_krope, layer, n_pages):
    b = q.shape[0]
    npg = PAGES_PER_STEP
    n_steps = n_pages // npg

    def page_spec(width, i):
        return pl.BlockSpec((1, 1, PAGE_SIZE, width),
                            lambda bi, j, pt: (layer, pt[bi * n_pages + j * npg + i], 0, 0))

    grid_spec = pltpu.PrefetchScalarGridSpec(
        num_scalar_prefetch=1,
        grid=(b, n_steps),
        in_specs=([pl.BlockSpec((1, MLA_HEADS, QCAT), lambda bi, j, pt: (bi, 0, 0)),
                   pl.BlockSpec((1, 1, QCAT), lambda bi, j, pt: (bi, 0, 0))]
                  + [page_spec(KV_LORA, i) for i in range(npg)]
                  + [page_spec(QK_ROPE_DIM, i) for i in range(npg)]),
        out_specs=pl.BlockSpec((1, MLA_HEADS, KV_LORA), lambda bi, j, pt: (bi, 0, 0)),
        scratch_shapes=[pltpu.VMEM((MLA_HEADS, 1), F32), pltpu.VMEM((MLA_HEADS, 1), F32),
                        pltpu.VMEM((MLA_HEADS, KV_LORA), F32)],
    )
    return pl.pallas_call(
        functools.partial(_mla_sample_kernel, n_steps=n_steps),
        grid_spec=grid_spec,
        out_shape=jax.ShapeDtypeStruct((b, MLA_HEADS, KV_LORA), BF16),
        compiler_params=_cp(("parallel", "arbitrary")),
        name="mla_sample",
    )(page_table_flat, q, k_new, *([cache_ckv] * npg), *([cache_krope] * npg))


def _cross_kernel(q_ref, mk_ref, mv_ref, o_ref, *, tq):
    scale = MEM_HEAD_DIM ** -0.5
    rows = max(tq, 8)
    for h in range(MEM_HEADS):
        hs = slice(h * MEM_HEAD_DIM, (h + 1) * MEM_HEAD_DIM)
        q = q_ref[0, :, hs]
        if tq < rows:
            q = jnp.broadcast_to(q, (rows, MEM_HEAD_DIM))
        kh = mk_ref[0, :, hs].astype(BF16)
        vh = mv_ref[0, :, hs].astype(BF16)
        s = _dot_nt(q, kh) * scale
        m = jnp.max(s, axis=1, keepdims=True)
        p = jnp.exp(s - m)
        l = jnp.sum(p, axis=1, keepdims=True)
        o = _dot(p.astype(BF16), vh) / l
        o_ref[0, :, hs] = o[:tq].astype(o_ref.dtype)


def cross_attend(q, mem_k, mem_v, tq):
    b, t, d = q.shape
    return pl.pallas_call(
        functools.partial(_cross_kernel, tq=tq),
        grid=(b, t // tq),
        in_specs=[pl.BlockSpec((1, tq, d), lambda i, j: (i, j, 0)),
                  pl.BlockSpec((1, N_MEM, d), lambda i, j: (i, 0, 0)),
                  pl.BlockSpec((1, N_MEM, d), lambda i, j: (i, 0, 0))],
        out_specs=pl.BlockSpec((1, tq, d), lambda i, j: (i, j, 0)),
        out_shape=jax.ShapeDtypeStruct((b, t, d), BF16),
        compiler_params=_cp(("parallel", "arbitrary")),
        name="cross_attend",
    )(q, mem_k, mem_v)


def _rot_cols(w):
    half = QK_ROPE_DIM // 2
    return jnp.concatenate([-w[..., half:], w[..., :half]], axis=-1)


def _pad_lanes(w):
    return jnp.pad(w, [(0, 0)] * (w.ndim - 1) + [(0, LANES - w.shape[-1])])


def _layer_weights(l, w_in, shift_mu, rwkv_w0, rwkv_w_up, rwkv_a0, rwkv_a_up, rwkv_g_up, rwkv_k_k, rwkv_k_a,
                   rwkv_r_k, rwkv_gn_w, rwkv_gn_b, mla_q_norm, mla_w_uq, mla_kv_norm, mla_w_uk, mla_w_uv):
    w3 = RWKV_WIDTH
    o_kr = RWKV_PROJ + Q_LORA + KV_LORA
    w_kr = w_in[l][:, o_kr:]
    w_in_ext = jnp.concatenate([w_in[l][:, :o_kr], _pad_lanes(w_kr), _pad_lanes(_rot_cols(w_kr))], axis=1).astype(BF16)
    w_lora = jnp.zeros((LORA_SLAB, 3 * w3), F32)
    w_lora = w_lora.at[:DECAY_LORA, :w3].set(rwkv_w_up[l])
    w_lora = w_lora.at[DECAY_LORA:DECAY_LORA + ICLR_LORA, w3:2 * w3].set(rwkv_a_up[l])
    w_lora = w_lora.at[DECAY_LORA + ICLR_LORA:, 2 * w3:].set(rwkv_g_up[l])
    uq = mla_w_uq[l].reshape(Q_LORA, MLA_HEADS, QK_NOPE_DIM + QK_ROPE_DIM)
    uq_nope = uq[:, :, :QK_NOPE_DIM].reshape(Q_LORA, MLA_HEADS * QK_NOPE_DIM)
    uq_rope = uq[:, :, QK_NOPE_DIM:]
    w_q = jnp.concatenate([uq_nope,
                           _pad_lanes(uq_rope).reshape(Q_LORA, MLA_HEADS * LANES),
                           _pad_lanes(_rot_cols(uq_rope)).reshape(Q_LORA, MLA_HEADS * LANES)], axis=1).astype(BF16)
    eye_h = jnp.eye(MLA_HEADS, dtype=F32)
    wuk_bd = jnp.einsum('chd,hg->hdgc', mla_w_uk[l], eye_h).reshape(MLA_HEADS * QK_NOPE_DIM, MLA_HEADS * KV_LORA).astype(BF16)
    wuv_bd = jnp.einsum('chd,hg->hcgd', mla_w_uv[l], eye_h).reshape(MLA_HEADS * KV_LORA, MLA_HEADS * V_HEAD_DIM).astype(BF16)
    row = lambda a: a.reshape(1, -1)
    return dict(
        w_in_ext=w_in_ext, mu=row(shift_mu[l]), w0=row(rwkv_w0[l]), a0=row(rwkv_a0[l]), w_lora=w_lora,
        k_k=row(rwkv_k_k[l]), k_a=row(rwkv_k_a[l]), rk=row(rwkv_r_k[l]), gnw=row(rwkv_gn_w[l]), gnb=row(rwkv_gn_b[l]),
        q_norm=row(mla_q_norm[l]), kv_norm=row(mla_kv_norm[l]), w_q=w_q, wuk_bd=wuk_bd, wuv_bd=wuv_bd)


def _rope_tables(pos):
    half = QK_ROPE_DIM // 2
    inv_freq = ROPE_THETA ** (-jnp.arange(half, dtype=F32) / half)
    ang = pos.astype(F32)[:, None] * inv_freq[None, :]
    cos, sin = jnp.cos(ang), jnp.sin(ang)
    return (_pad_lanes(jnp.concatenate([cos, cos], axis=1)), _pad_lanes(jnp.concatenate([sin, sin], axis=1)))


def kernel(x_prompt, x_sample, cache_ckv, cache_krope, state_wkv, state_shift, cache_mem_k, cache_mem_v, page_table, mem_prompt, norm_ffn1, ffn1_w_gate, ffn1_w_up, ffn1_w_down, norm_mix, w_in, shift_mu, rwkv_w0, rwkv_w_up, rwkv_a0, rwkv_a_up, rwkv_g_up, rwkv_k_k, rwkv_k_a, rwkv_r_k, rwkv_gn_w, rwkv_gn_b, mla_q_norm, mla_w_uq, mla_kv_norm, mla_w_uk, mla_w_uv, w_out, norm_cross, norm_mem, mem_w_q, mem_w_k, mem_w_v, mem_w_o, norm_ffn2, ffn2_w_gate, ffn2_w_up, ffn2_w_down, norm_final):
    bp, tp, d = x_prompt.shape
    bs, ts, _ = x_sample.shape
    assert ts == 1
    n_pages = page_table.shape[1]
    past_len = n_pages * PAGE_SIZE
    hh, nn = RWKV_HEADS, RWKV_HEAD_DIM
    row = lambda a: a.reshape(1, -1)
    bf = lambda a: a.astype(BF16)

    head_of = jnp.arange(RWKV_WIDTH) // nn
    ones_bd = (head_of[:, None] == head_of[None, :]).astype(F32)
    cos_p, sin_p = _rope_tables(jnp.arange(tp))
    cos_s, sin_s = _rope_tables(jnp.full((bs,), past_len, jnp.int32))
    pt_flat = page_table.reshape(-1)

    lw = [_layer_weights(l, w_in, shift_mu, rwkv_w0, rwkv_w_up, rwkv_a0, rwkv_a_up, rwkv_g_up, rwkv_k_k, rwkv_k_a,
                         rwkv_r_k, rwkv_gn_w, rwkv_gn_b, mla_q_norm, mla_w_uq, mla_kv_norm, mla_w_uk, mla_w_uv)
          for l in range(DEPTH)]

    tm_p, tm_s = 512, bs
    tf = D_FF // 2

    def run_ffn(x2, l, which, tm, final):
        if which == 1:
            g, wg, wu, wd = norm_ffn1[l], ffn1_w_gate[l], ffn1_w_up[l], ffn1_w_down[l]
        else:
            g, wg, wu, wd = norm_ffn2[l], ffn2_w_gate[l], ffn2_w_up[l], ffn2_w_down[l]
        return ffn(x2, row(g), bf(wg), bf(wu), bf(wd), row(norm_final), final, tm, tf)

    mem2 = mem_prompt.reshape(bp * N_MEM, d)
    mem_k_p, mem_v_p = [], []
    for l in range(DEPTH):
        kv = norm_matmul(mem2, row(norm_mem[l]), bf(jnp.concatenate([mem_w_k[l], mem_w_v[l]], axis=1)), F32, 512)
        mem_k_p.append(kv[:, :d].reshape(bp, N_MEM, d))
        mem_v_p.append(kv[:, d:].reshape(bp, N_MEM, d))

    def trunk(x, sample):
        b, t, _ = x.shape
        m = b * t
        tm = tm_s if sample else tm_p
        x2 = x.reshape(m, d)
        ckvs, krs, wkvs, shifts = [], [], [], []
        for l in range(DEPTH):
            p = lw[l]
            x2 = run_ffn(x2, l, 1, tm, False)
            proj = norm_matmul(x2, row(norm_mix[l]), p['w_in_ext'], F32, tm)
            if sample:
                outs = prep(proj.reshape(1, m, IN_PROJ_EXT), state_shift[l].reshape(1, m, RWKV_PROJ), True,
                            p['mu'], p['w0'], p['a0'], p['w_lora'], p['k_k'], p['k_a'], ones_bd, p['q_norm'],
                            p['kv_norm'], p['w_q'], p['wuk_bd'], cos_s, sin_s, m)
                shift_last = proj[:, :RWKV_PROJ]
            else:
                outs = prep(proj.reshape(b, t, IN_PROJ_EXT), jnp.zeros((b, 1, RWKV_PROJ), F32), False,
                            p['mu'], p['w0'], p['a0'], p['w_lora'], p['k_k'], p['k_a'], ones_bd, p['q_norm'],
                            p['kv_norm'], p['w_q'], p['wuk_bd'], cos_p, sin_p, 256)
                shift_last = outs[11].reshape(b, RWKV_PROJ)
            r, k, v, kk, bv, lwd, g, ckv, kr, q_cat, k_cat = outs[:11]
            if sample:
                to_heads = lambda a: a.reshape(m, hh, nn)
                o_rwkv, wkv = rwkv_step(to_heads(r), to_heads(k), to_heads(v), to_heads(kk), to_heads(bv),
                                        to_heads(lwd), to_heads(g), p['rk'].reshape(hh, nn), p['gnw'].reshape(hh, nn),
                                        p['gnb'].reshape(hh, nn), state_wkv[l], 8)
                o_rwkv = bf(o_rwkv.reshape(m, RWKV_WIDTH))
                o_lat = mla_sample(pt_flat, q_cat.reshape(m, MLA_HEADS, QCAT), k_cat.reshape(m, 1, QCAT),
                                   cache_ckv, cache_krope, l, n_pages).reshape(m, MLA_HEADS * KV_LORA)
                mk, mv = cache_mem_k[l].reshape(b, N_MEM, d), cache_mem_v[l].reshape(b, N_MEM, d)
            else:
                o_rwkv, s_bd = rwkv_chunked(r, k, v, kk, bv, lwd, g, p['rk'], p['gnw'], p['gnb'], CHUNK)
                o_rwkv = o_rwkv.reshape(m, RWKV_WIDTH)
                wkv = jnp.stack([s_bd[:, :, :nn, :nn], s_bd[:, :, nn:, nn:]], axis=2).reshape(b, hh, nn, nn)
                o_lat = mla_prompt(q_cat, k_cat, 256, 256).reshape(m, MLA_HEADS * KV_LORA)
                mk, mv = mem_k_p[l], mem_v_p[l]
            x2 = mix_out(o_rwkv, o_lat, p['wuv_bd'], bf(w_out[l]), x2, tm)
            q = norm_matmul(x2, row(norm_cross[l]), bf(mem_w_q[l]), BF16, tm)
            att = cross_attend(q.reshape(b, t, d), mk, mv, 1 if sample else 512)
            x2 = matmul_res(att.reshape(m, d), bf(mem_w_o[l]), x2, tm)
            x2 = run_ffn(x2, l, 2, tm, l == DEPTH - 1)
            ckvs.append(ckv.reshape(b, t, KV_LORA))
            krs.append(kr.reshape(b, t, QK_ROPE_DIM))
            wkvs.append(wkv)
            shifts.append(shift_last)
        return x2.reshape(b, t, d), jnp.stack(ckvs), jnp.stack(krs), jnp.stack(wkvs), jnp.stack(shifts)

    y_p, ckv_p, kr_p, wkv_p, shift_p = trunk(x_prompt, False)
    y_s, ckv_s, kr_s, wkv_s, shift_s = trunk(x_sample, True)
    mem_shape = (DEPTH, bp, N_MEM, MEM_HEADS, MEM_HEAD_DIM)
    return (y_p, y_s, ckv_p, kr_p, wkv_p, shift_p,
            jnp.stack(mem_k_p).reshape(mem_shape), jnp.stack(mem_v_p).reshape(mem_shape),
            ckv_s, kr_s, wkv_s, shift_s)
```

```python
import functools

import jax
import jax.numpy as jnp
from jax import lax
from jax.experimental import pallas as pl
from jax.experimental.pallas import tpu as pltpu

F32 = jnp.float32
BF16 = jnp.bfloat16
HI = lax.Precision.HIGHEST

D_MODEL = 1024
DEPTH = 2
PAGE_SIZE = 128
RWKV_HEADS = 8
RWKV_HEAD_DIM = 64
RWKV_WIDTH = 512
DECAY_LORA = 64
ICLR_LORA = 64
GATE_LORA = 128
GN_EPS = 64e-5
MLA_HEADS = 8
QK_NOPE_DIM = 64
QK_ROPE_DIM = 32
V_HEAD_DIM = 64
Q_LORA = 256
KV_LORA = 128
ROPE_THETA = 10000.0
MLA_SCALE = (QK_NOPE_DIM + QK_ROPE_DIM) ** -0.5
RWKV_PROJ = 3 * RWKV_WIDTH + DECAY_LORA + ICLR_LORA + GATE_LORA
N_MEM = 256
MEM_HEADS = 4
MEM_HEAD_DIM = D_MODEL // MEM_HEADS
D_FF = 2816
RMS_EPS = 1e-6

LANES = 128
IN_PROJ_EXT = RWKV_PROJ + Q_LORA + KV_LORA + 2 * LANES
LORA_SLAB = DECAY_LORA + ICLR_LORA + GATE_LORA
QCAT = 2 * LANES
CHUNK = 64
PAGE_GROUP = 32
NEG = -1e30
VMEM_LIMIT = 56 * 1024 * 1024


def _cp(sem):
    return pltpu.CompilerParams(dimension_semantics=sem, vmem_limit_bytes=VMEM_LIMIT)


def _dot(a, b, prec=None):
    return jnp.dot(a, b, preferred_element_type=F32, precision=prec)


def _dot_nt(a, b, prec=None):
    return lax.dot_general(a, b, (((1,), (1,)), ((), ())), preferred_element_type=F32, precision=prec)


def _bf(x):
    return x.astype(BF16)


def _split3(x):
    hi = _bf(x)
    r1 = x - hi.astype(F32)
    mid = _bf(r1)
    return hi, mid, _bf(r1 - mid.astype(F32))


def _rms(x, g):
    return x * lax.rsqrt(jnp.mean(x * x, axis=-1, keepdims=True) + RMS_EPS) * g


def _sigmoid(x):
    return 1.0 / (1.0 + jnp.exp(-x))


def _norm_matmul_kernel(x_ref, g_ref, w_ref, o_ref):
    h = _bf(_rms(x_ref[...], g_ref[...]))
    o_ref[...] = _dot(h, w_ref[...]).astype(o_ref.dtype)


def norm_matmul(x, g, w, out_dtype, tm):
    m, d = x.shape
    n = w.shape[1]
    return pl.pallas_call(
        _norm_matmul_kernel,
        grid=(m // tm,),
        in_specs=[pl.BlockSpec((tm, d), lambda i: (i, 0)),
                  pl.BlockSpec((1, d), lambda i: (0, 0)),
                  pl.BlockSpec((d, n), lambda i: (0, 0))],
        out_specs=pl.BlockSpec((tm, n), lambda i: (i, 0)),
        out_shape=jax.ShapeDtypeStruct((m, n), out_dtype),
        compiler_params=_cp(("parallel",)),
        name="norm_matmul",
    )(x, g, w)


def _ffn_kernel(x_ref, g_ref, wg_ref, wu_ref, wd_ref, gf_ref, o_ref, h_scr, acc_scr, *, nf, final_norm):
    f = pl.program_id(1)

    @pl.when(f == 0)
    def _():
        h_scr[...] = _bf(_rms(x_ref[...], g_ref[...]))
        acc_scr[...] = jnp.zeros_like(acc_scr)

    h = h_scr[...]
    gate = _dot(h, wg_ref[...])
    up = _dot(h, wu_ref[...])
    act = _bf(gate * _sigmoid(gate) * up)
    acc_scr[...] += _dot(act, wd_ref[...])

    @pl.when(f == nf - 1)
    def _():
        y = x_ref[...] + 0.5 * acc_scr[...]
        if final_norm:
            y = _rms(y, gf_ref[...])
        o_ref[...] = y


def ffn(x, g, wg, wu, wd, gf, final_norm, tm, tf):
    m, d = x.shape
    dff = wg.shape[1]
    nf = dff // tf
    return pl.pallas_call(
        functools.partial(_ffn_kernel, nf=nf, final_norm=final_norm),
        grid=(m // tm, nf),
        in_specs=[pl.BlockSpec((tm, d), lambda i, f: (i, 0)),
                  pl.BlockSpec((1, d), lambda i, f: (0, 0)),
                  pl.BlockSpec((d, tf), lambda i, f: (0, f)),
                  pl.BlockSpec((d, tf), lambda i, f: (0, f)),
                  pl.BlockSpec((tf, d), lambda i, f: (f, 0)),
                  pl.BlockSpec((1, d), lambda i, f: (0, 0))],
        out_specs=pl.BlockSpec((tm, d), lambda i, f: (i, 0)),
        out_shape=jax.ShapeDtypeStruct((m, d), F32),
        scratch_shapes=[pltpu.VMEM((tm, d), BF16), pltpu.VMEM((tm, d), F32)],
        compiler_params=_cp(("parallel", "arbitrary")),
        name="ffn",
    )(x, g, wg, wu, wd, gf)


def _matmul_res_kernel(a_ref, w_ref, x_ref, o_ref):
    o_ref[...] = x_ref[...] + _dot(a_ref[...], w_ref[...])


def matmul_res(a, w, x, tm):
    m, k = a.shape
    n = w.shape[1]
    return pl.pallas_call(
        _matmul_res_kernel,
        grid=(m // tm,),
        in_specs=[pl.BlockSpec((tm, k), lambda i: (i, 0)),
                  pl.BlockSpec((k, n), lambda i: (0, 0)),
                  pl.BlockSpec((tm, n), lambda i: (i, 0))],
        out_specs=pl.BlockSpec((tm, n), lambda i: (i, 0)),
        out_shape=jax.ShapeDtypeStruct((m, n), F32),
        compiler_params=_cp(("parallel",)),
        name="matmul_res",
    )(a, w, x)


def _mix_out_kernel(orw_ref, olat_ref, wuv_ref, wout_ref, x_ref, o_ref):
    o_mla = _bf(_dot(olat_ref[...], wuv_ref[...]))
    y = _dot(orw_ref[...], wout_ref[:RWKV_WIDTH, :]) + _dot(o_mla, wout_ref[RWKV_WIDTH:, :])
    o_ref[...] = x_ref[...] + y


def mix_out(o_rwkv, o_lat, wuv_bd, w_out, x, tm):
    m, d = x.shape
    return pl.pallas_call(
        _mix_out_kernel,
        grid=(m // tm,),
        in_specs=[pl.BlockSpec((tm, RWKV_WIDTH), lambda i: (i, 0)),
                  pl.BlockSpec((tm, MLA_HEADS * KV_LORA), lambda i: (i, 0)),
                  pl.BlockSpec(wuv_bd.shape, lambda i: (0, 0)),
                  pl.BlockSpec(w_out.shape, lambda i: (0, 0)),
                  pl.BlockSpec((tm, d), lambda i: (i, 0))],
        out_specs=pl.BlockSpec((tm, d), lambda i: (i, 0)),
        out_shape=jax.ShapeDtypeStruct((m, d), F32),
        compiler_params=_cp(("parallel",)),
        name="mix_out",
    )(o_rwkv, o_lat, wuv_bd, w_out, x)


def _prep_kernel(proj_ref, prev_ref, mu_ref, w0_ref, a0_ref, wl_ref, kk_ref, ka_ref, ones_ref,
                 qn_ref, kvn_ref, wq_ref, wuk_ref, cos_ref, sin_ref,
                 r_o, k_o, v_o, kkn_o, b_o, lw_o, g_o, ckv_o, kr_o, qcat_o, kcat_o, last_o,
                 carry, *, tm, explicit_prev):
    p = proj_ref[0, :, :RWKV_PROJ]
    if explicit_prev:
        prev = prev_ref[0]
    else:
        @pl.when(pl.program_id(1) == 0)
        def _():
            carry[...] = prev_ref[0]

        row = lax.broadcasted_iota(jnp.int32, (tm, 1), 0)
        prev = jnp.where(row == 0, carry[...], pltpu.roll(p, 1, axis=0))
        carry[...] = p[tm - 1:tm, :]
    last_o[0] = p[tm - 1:tm, :]
    ps = p + (prev - p) * mu_ref[...]

    w3 = RWKV_WIDTH
    r = ps[:, :w3]
    k = ps[:, w3:2 * w3]
    v = ps[:, 2 * w3:3 * w3]
    slab = ps[:, 3 * w3:]
    lane = lax.broadcasted_iota(jnp.int32, slab.shape, 1)
    act = jnp.where(lane < DECAY_LORA, jnp.tanh(slab),
                    jnp.where(lane < DECAY_LORA + ICLR_LORA, slab, _sigmoid(slab)))
    lo = _dot(act, wl_ref[...], HI)
    z = -(w0_ref[...] + lo[:, :w3])
    softplus = jnp.maximum(z, 0.0) + jnp.log(1.0 + jnp.exp(-jnp.abs(z)))
    lw = -jnp.exp(-softplus - 0.5)
    a = _sigmoid(a0_ref[...] + lo[:, w3:2 * w3])
    g = lo[:, 2 * w3:]
    kkr = k * kk_ref[...]
    ssq = _dot(kkr * kkr, ones_ref[...], HI)
    kkn = kkr / jnp.maximum(jnp.sqrt(ssq), 1e-12)
    r_o[0] = r
    k_o[0] = k * (1.0 + (a - 1.0) * ka_ref[...])
    v_o[0] = v
    kkn_o[0] = kkn
    b_o[0] = kkn * a
    lw_o[0] = lw
    g_o[0] = g

    o1 = RWKV_PROJ
    cq = _bf(_rms(proj_ref[0, :, o1:o1 + Q_LORA], qn_ref[...]))
    o2 = o1 + Q_LORA
    ckv = _rms(proj_ref[0, :, o2:o2 + KV_LORA], kvn_ref[...])
    o3 = o2 + KV_LORA
    cos = cos_ref[...]
    sin = sin_ref[...]
    kr = proj_ref[0, :, o3:o3 + LANES] * cos + proj_ref[0, :, o3 + LANES:o3 + 2 * LANES] * sin
    ckv_o[0] = ckv
    kr_o[0] = kr[:, :QK_ROPE_DIM]
    kcat_o[0] = _bf(jnp.concatenate([ckv, kr], axis=1))

    qq = _dot(cq, wq_ref[...])
    nn = MLA_HEADS * QK_NOPE_DIM
    nr = MLA_HEADS * LANES
    q_lat = _dot(_bf(qq[:, :nn]), wuk_ref[...]) * MLA_SCALE
    cos8 = jnp.concatenate([cos] * MLA_HEADS, axis=1)
    sin8 = jnp.concatenate([sin] * MLA_HEADS, axis=1)
    q_rp = (qq[:, nn:nn + nr] * cos8 + qq[:, nn + nr:] * sin8) * MLA_SCALE
    pieces = []
    for h in range(MLA_HEADS):
        pieces.append(q_lat[:, h * LANES:(h + 1) * LANES])
        pieces.append(q_rp[:, h * LANES:(h + 1) * LANES])
    qcat_o[0] = _bf(jnp.concatenate(pieces, axis=1))


def prep(proj, prev, explicit_prev, mu, w0, a0, w_lora, k_k, k_a, ones_bd, q_norm, kv_norm, w_q, wuk_bd, cos_t, sin_t, tm):
    b, t, n = proj.shape
    nt = t // tm
    wide = lambda w: pl.BlockSpec((1, tm, w), lambda i, j: (i, j, 0))
    full = lambda arr: pl.BlockSpec(arr.shape, lambda i, j: (0,) * arr.ndim)
    if explicit_prev:
        prev_spec = pl.BlockSpec((1, tm, RWKV_PROJ), lambda i, j: (i, j, 0))
    else:
        prev_spec = pl.BlockSpec((1, 1, RWKV_PROJ), lambda i, j: (i, 0, 0))
    sd = lambda w, dt: jax.ShapeDtypeStruct((b, t, w), dt)
    w3 = RWKV_WIDTH
    out_shape = [sd(w3, F32)] * 7 + [sd(KV_LORA, F32), sd(QK_ROPE_DIM, F32),
                                      sd(MLA_HEADS * QCAT, BF16), sd(QCAT, BF16),
                                      jax.ShapeDtypeStruct((b, 1, RWKV_PROJ), F32)]
    out_specs = [wide(w3)] * 7 + [wide(KV_LORA), wide(QK_ROPE_DIM), wide(MLA_HEADS * QCAT), wide(QCAT),
                                   pl.BlockSpec((1, 1, RWKV_PROJ), lambda i, j: (i, 0, 0))]
    return pl.pallas_call(
        functools.partial(_prep_kernel, tm=tm, explicit_prev=explicit_prev),
        grid=(b, nt),
        in_specs=[wide(n), prev_spec, full(mu), full(w0), full(a0), full(w_lora), full(k_k), full(k_a),
                  full(ones_bd), full(q_norm), full(kv_norm), full(w_q), full(wuk_bd),
                  pl.BlockSpec((tm, LANES), lambda i, j: (j, 0)),
                  pl.BlockSpec((tm, LANES), lambda i, j: (j, 0))],
        out_specs=out_specs,
        out_shape=out_shape,
        scratch_shapes=[pltpu.VMEM((1, RWKV_PROJ), F32)],
        compiler_params=_cp(("arbitrary", "arbitrary")),
        name="prep",
    )(proj, prev, mu, w0, a0, w_lora, k_k, k_a, ones_bd, q_norm, kv_norm, w_q, wuk_bd, cos_t, sin_t)


def _rwkv_chunk_kernel(r_ref, k_ref, v_ref, kk_ref, b_ref, lw_ref, g_ref, rk_ref, gnw_ref, gnb_ref,
                       o_ref, s_out_ref, s_scr, *, c, n_chunks):
    @pl.when(pl.program_id(1) == 0)
    def _():
        s_scr[...] = jnp.zeros_like(s_scr)

    n = RWKV_HEAD_DIM
    npair = RWKV_HEADS // 2
    row = lax.broadcasted_iota(jnp.int32, (c, c), 0)
    col = lax.broadcasted_iota(jnp.int32, (c, c), 1)
    incl = row >= col
    strict = row > col
    tri = _bf(jnp.where(incl, 1.0, 0.0))
    eye = jnp.where(row == col, 1.0, 0.0).astype(F32)
    lane = lax.broadcasted_iota(jnp.int32, (c, LANES), 1)
    is_e = lane < n
    r2 = lax.broadcasted_iota(jnp.int32, (LANES, LANES), 0)
    c2 = lax.broadcasted_iota(jnp.int32, (LANES, LANES), 1)
    same_head = (r2 < n) == (c2 < n)
    ones_bd = _bf(jnp.where(same_head, 1.0, 0.0))
    n_double = c.bit_length() - 2
    pairs = range(npair)
    heads = [(j, x) for j in pairs for x in range(2)]
    sl = [slice(j * LANES, (j + 1) * LANES) for j in pairs]

    r = [r_ref[0, :, sl[j]] for j in pairs]
    k = [k_ref[0, :, sl[j]] for j in pairs]
    v = [v_ref[0, :, sl[j]] for j in pairs]
    kk = [kk_ref[0, :, sl[j]] for j in pairs]
    bv = [b_ref[0, :, sl[j]] for j in pairs]
    lw = [lw_ref[0, :, sl[j]] for j in pairs]
    s0 = [s_scr[j] for j in pairs]

    cum = [sum(_dot(tri, part) for part in _split3(lw[j])) for j in pairs]
    e_pos = [jnp.exp(cum[j]) for j in pairs]
    e_neg = [jnp.exp(-cum[j]) for j in pairs]
    a_t = [-kk[j] * jnp.exp(cum[j] - lw[j]) for j in pairs]
    r_t = [r[j] * e_pos[j] for j in pairs]
    b_t = [bv[j] * e_neg[j] for j in pairs]
    k_t = [k[j] * e_neg[j] for j in pairs]
    g_c = [e_pos[j][c - 1:c, :] for j in pairs]

    v_b = [_bf(v[j]) for j in pairs]
    s0_b = [_bf(s0[j]) for j in pairs]
    zero = jnp.zeros((c, LANES), F32)
    xb, xk = [], []
    for j in pairs:
        lhs = _bf(jnp.concatenate([jnp.where(is_e, a_t[j], zero), jnp.where(is_e, r_t[j], zero),
                                   jnp.where(is_e, zero, a_t[j]), jnp.where(is_e, zero, r_t[j])], axis=0))
        xb.append(_dot_nt(lhs, _bf(b_t[j])))
        xk.append(_dot_nt(lhs, _bf(k_t[j])))
    rs0 = [_dot_nt(_bf(r_t[j]), s0_b[j]) for j in pairs]

    nmat, arb, aakv, o2 = {}, {}, {}, {}
    for (j, x) in heads:
        base = 2 * c * x
        nmat[j, x] = jnp.where(strict, xb[j][base:base + c], 0.0)
        arb[j, x] = _bf(jnp.where(incl, xb[j][base + c:base + 2 * c], 0.0))
        aakv[j, x] = _dot(_bf(jnp.where(strict, xk[j][base:base + c], 0.0)), v_b[j])
        o2[j, x] = _dot(_bf(jnp.where(incl, xk[j][base + c:base + 2 * c], 0.0)), v_b[j])

    tinv = {h: eye + nmat[h] for h in heads}
    pw = dict(nmat)
    for _ in range(n_double):
        pw = {h: _dot(_bf(pw[h]), _bf(pw[h])) for h in heads}
        tinv = {h: tinv[h] + _dot(_bf(pw[h]), _bf(tinv[h])) for h in heads}

    w12 = {h: _dot(_bf(tinv[h]), _bf(jnp.concatenate([a_t[h[0]], aakv[h]], axis=1))) for h in heads}
    u = []
    for j in pairs:
        w1 = jnp.where(is_e, w12[j, 0][:, :LANES], w12[j, 1][:, :LANES])
        w2 = jnp.where(is_e, w12[j, 0][:, LANES:], w12[j, 1][:, LANES:])
        u.append(_dot_nt(_bf(w1), s0_b[j]) + w2)
    u_b = [_bf(u[j]) for j in pairs]
    o = [rs0[j] + jnp.where(is_e, _dot(arb[j, 0], u_b[j]) + o2[j, 0], _dot(arb[j, 1], u_b[j]) + o2[j, 1])
         for j in pairs]
    for j in pairs:
        uv = jnp.concatenate([u[j], v[j]], axis=0)
        bk = jnp.concatenate([b_t[j] * g_c[j], k_t[j] * g_c[j]], axis=0)
        s_scr[j] = s0[j] * g_c[j] + jnp.where(same_head, _dot(_bf(uv.T), _bf(bk)), 0.0)

    inv_n = 1.0 / n
    o_all = jnp.concatenate(o, axis=0)
    mean = _dot(_bf(o_all), ones_bd) * inv_n
    d = o_all - mean
    var = _dot(_bf(d * d), ones_bd) * inv_n
    bsum = _dot(_bf(jnp.concatenate([r[j] * k[j] * rk_ref[:, sl[j]] for j in pairs], axis=0)), ones_bd)
    on = d * lax.rsqrt(var + GN_EPS)
    for j in pairs:
        rows = slice(j * c, (j + 1) * c)
        out = (on[rows] * gnw_ref[:, sl[j]] + gnb_ref[:, sl[j]] + bsum[rows] * v[j]) * g_ref[0, :, sl[j]]
        o_ref[0, :, sl[j]] = out.astype(o_ref.dtype)

    @pl.when(pl.program_id(1) == n_chunks - 1)
    def _():
        s_out_ref[0] = s_scr[...]


def rwkv_chunked(r, k, v, kk, bv, lw, g, rk, gnw, gnb, c):
    b, t, w = r.shape
    n_chunks = t // c
    seq = pl.BlockSpec((1, c, w), lambda i, j: (i, j, 0))
    par = pl.BlockSpec((1, w), lambda i, j: (0, 0))
    npair = RWKV_HEADS // 2
    return pl.pallas_call(
        functools.partial(_rwkv_chunk_kernel, c=c, n_chunks=n_chunks),
        grid=(b, n_chunks),
        in_specs=[seq] * 7 + [par] * 3,
        out_specs=[seq, pl.BlockSpec((1, npair, LANES, LANES), lambda i, j: (i, 0, 0, 0))],
        out_shape=[jax.ShapeDtypeStruct((b, t, w), BF16),
                   jax.ShapeDtypeStruct((b, npair, LANES, LANES), F32)],
        scratch_shapes=[pltpu.VMEM((npair, LANES, LANES), F32)],
        compiler_params=_cp(("arbitrary", "arbitrary")),
        name="rwkv_chunked",
    )(r, k, v, kk, bv, lw, g, rk, gnw, gnb)


def _rwkv_step_kernel(r_ref, k_ref, v_ref, kk_ref, b_ref, lw_ref, g_ref, rk_ref, gnw_ref, gnb_ref, s_ref,
                      o_ref, s_out_ref, *, bb):
    n = RWKV_HEAD_DIM
    eye = lax.broadcasted_iota(jnp.int32, (n, n), 0) == lax.broadcasted_iota(jnp.int32, (n, n), 1)

    def body(i, carry):
        for h in range(RWKV_HEADS):
            hs = slice(h, h + 1)
            s = s_ref[i, h]
            r = r_ref[i, hs, :]
            k = k_ref[i, hs, :]
            v = v_ref[i, hs, :]
            kk = kk_ref[i, hs, :]
            bv = b_ref[i, hs, :]
            w = jnp.exp(lw_ref[i, hs, :])
            s_kk = jnp.sum(s * kk, axis=1, keepdims=True)
            v_col = jnp.sum(jnp.where(eye, v, 0.0), axis=1, keepdims=True)
            s_new = s * w - s_kk * bv + v_col * k
            o_col = jnp.sum(s_new * r, axis=1, keepdims=True)
            o = jnp.sum(jnp.where(eye, o_col, 0.0), axis=0, keepdims=True)
            mean = jnp.mean(o, axis=1, keepdims=True)
            d = o - mean
            var = jnp.mean(d * d, axis=1, keepdims=True)
            on = d * lax.rsqrt(var + GN_EPS) * gnw_ref[hs, :] + gnb_ref[hs, :]
            bonus = jnp.sum(r * k * rk_ref[hs, :], axis=1, keepdims=True) * v
            o_ref[i, hs, :] = ((on + bonus) * g_ref[i, hs, :]).astype(o_ref.dtype)
            s_out_ref[i, h] = s_new
        return carry

    lax.fori_loop(0, bb, body, 0)


def rwkv_step(r, k, v, kk, bv, lw, g, rk, gnw, gnb, s, bb):
    b = r.shape[0]
    h, n = RWKV_HEADS, RWKV_HEAD_DIM
    vec = pl.BlockSpec((bb, h, n), lambda i: (i, 0, 0))
    par = pl.BlockSpec((h, n), lambda i: (0, 0))
    st = pl.BlockSpec((bb, h, n, n), lambda i: (i, 0, 0, 0))
    return pl.pallas_call(
        functools.partial(_rwkv_step_kernel, bb=bb),
        grid=(b // bb,),
        in_specs=[vec] * 7 + [par] * 3 + [st],
        out_specs=[vec, st],
        out_shape=[jax.ShapeDtypeStruct((b, h, n), F32), jax.ShapeDtypeStruct((b, h, n, n), F32)],
        compiler_params=_cp(("parallel",)),
        name="rwkv_step",
    )(r, k, v, kk, bv, lw, g, rk, gnw, gnb, s)


def _mla_prompt_kernel(q_ref, k_ref, o_ref, m_scr, l_scr, acc_scr, *, tq, tk):
    i = pl.program_id(1)
    nh = MLA_HEADS
    q = jnp.concatenate([q_ref[0, :, h * QCAT:(h + 1) * QCAT] for h in range(nh)], axis=0)
    row = lax.broadcasted_iota(jnp.int32, (tq, 1), 0)
    q_pos = i * tq + jnp.concatenate([row] * nh, axis=0)
    k_iota = lax.broadcasted_iota(jnp.int32, (1, tk), 1)
    n_kv = (i * tq + tq + tk - 1) // tk
    m_scr[...] = jnp.full_like(m_scr, NEG)
    l_scr[...] = jnp.zeros_like(l_scr)
    acc_scr[...] = jnp.zeros_like(acc_scr)

    def body(j, carry):
        kb = k_ref[0, pl.ds(pl.multiple_of(j * tk, tk), tk), :]
        s = _dot_nt(q, kb)
        s = jnp.where(j * tk + k_iota <= q_pos, s, NEG)
        m_old = m_scr[...]
        m_new = jnp.maximum(m_old, jnp.max(s, axis=1, keepdims=True))
        p = jnp.exp(s - m_new)
        alpha = jnp.exp(m_old - m_new)
        l_scr[...] = alpha * l_scr[...] + jnp.sum(p, axis=1, keepdims=True)
        acc_scr[...] = alpha * acc_scr[...] + _dot(_bf(p), kb[:, :KV_LORA])
        m_scr[...] = m_new
        return carry

    lax.fori_loop(0, n_kv, body, 0)
    o = acc_scr[...] / l_scr[...]
    for h in range(nh):
        o_ref[0, :, h * KV_LORA:(h + 1) * KV_LORA] = o[h * tq:(h + 1) * tq].astype(o_ref.dtype)


def mla_prompt(q_cat, k_cat, tq, tk):
    b, t, _ = q_cat.shape
    rows = MLA_HEADS * tq
    return pl.pallas_call(
        functools.partial(_mla_prompt_kernel, tq=tq, tk=tk),
        grid=(b, t // tq),
        in_specs=[pl.BlockSpec((1, tq, MLA_HEADS * QCAT), lambda i, j: (i, j, 0)),
                  pl.BlockSpec((1, t, QCAT), lambda i, j: (i, 0, 0))],
        out_specs=pl.BlockSpec((1, tq, MLA_HEADS * KV_LORA), lambda i, j: (i, j, 0)),
        out_shape=jax.ShapeDtypeStruct((b, t, MLA_HEADS * KV_LORA), BF16),
        scratch_shapes=[pltpu.VMEM((rows, 1), F32), pltpu.VMEM((rows, 1), F32), pltpu.VMEM((rows, KV_LORA), F32)],
        compiler_params=_cp(("parallel", "arbitrary")),
        name="mla_prompt",
    )(q_cat, k_cat)


def _mla_sample_kernel(pt_ref, q_ref, knew_ref, ckv_hbm, krt_hbm, o_ref, ckv_buf, kr_buf, s_scr, sem,
                       *, layer, n_pages, n_seq):
    b = pl.program_id(0)
    slot = b % 2

    def page_copies(seq, sl, i, page):
        return (pltpu.make_async_copy(ckv_hbm.at[layer, page], ckv_buf.at[sl, i], sem.at[sl, 0]),
                pltpu.make_async_copy(krt_hbm.at[layer, page], kr_buf.at[sl, i], sem.at[sl, 1]))

    def fetch(seq, sl):
        def body(i, carry):
            for cp in page_copies(seq, sl, i, pt_ref[seq * n_pages + i]):
                cp.start()
            return carry
        lax.fori_loop(0, n_pages, body, 0)

    @pl.when(b == 0)
    def _():
        fetch(0, 0)

    @pl.when(b + 1 < n_seq)
    def _():
        fetch(b + 1, 1 - slot)

    def wait_body(i, carry):
        for cp in page_copies(b, slot, i, 0):
            cp.wait()
        return carry
    lax.fori_loop(0, n_pages, wait_body, 0)

    q = q_ref[0]
    ql = q[:, :KV_LORA]
    qr = q[:, KV_LORA:KV_LORA + QK_ROPE_DIM]
    n_groups = n_pages // PAGE_GROUP

    def score_body(gi, carry):
        for i in range(PAGE_GROUP):
            pi = gi * PAGE_GROUP + i
            s_scr[pi] = _dot_nt(ql, _bf(ckv_buf[slot, pi])) + _dot(qr, _bf(kr_buf[slot, pi]))
        return carry
    lax.fori_loop(0, n_groups, score_body, 0)

    k_new = knew_ref[0].astype(F32)
    s_new = jnp.sum(q.astype(F32) * k_new, axis=1, keepdims=True)
    s = s_scr[...]
    m = jnp.maximum(jnp.max(jnp.max(s, axis=0), axis=1, keepdims=True), s_new)
    p = jnp.exp(s - m)
    p_new = jnp.exp(s_new - m)
    l = jnp.sum(jnp.sum(p, axis=0), axis=1, keepdims=True) + p_new
    s_scr[...] = p

    def pv_body(gi, acc):
        for i in range(PAGE_GROUP):
            pi = gi * PAGE_GROUP + i
            acc = acc + _dot(_bf(s_scr[pi]), _bf(ckv_buf[slot, pi]))
        return acc
    acc = lax.fori_loop(0, n_groups, pv_body, jnp.zeros((MLA_HEADS, KV_LORA), F32))
    o_ref[0] = ((acc + p_new * k_new[:, :KV_LORA]) / l).astype(o_ref.dtype)


def mla_sample(page_table_flat, q, k_new, cache_ckv, cache_krope_t, layer, n_pages):
    n_seq = q.shape[0]
    grid_spec = pltpu.PrefetchScalarGridSpec(
        num_scalar_prefetch=1,
        grid=(n_seq,),
        in_specs=[pl.BlockSpec((1, MLA_HEADS, QCAT), lambda bi, pt: (bi, 0, 0)),
                  pl.BlockSpec((1, 1, QCAT), lambda bi, pt: (bi, 0, 0)),
                  pl.BlockSpec(memory_space=pl.ANY),
                  pl.BlockSpec(memory_space=pl.ANY)],
        out_specs=pl.BlockSpec((1, MLA_HEADS, KV_LORA), lambda bi, pt: (bi, 0, 0)),
        scratch_shapes=[pltpu.VMEM((2, n_pages, PAGE_SIZE, KV_LORA), F32),
                        pltpu.VMEM((2, n_pages, QK_ROPE_DIM, PAGE_SIZE), F32),
                        pltpu.VMEM((n_pages, MLA_HEADS, PAGE_SIZE), F32),
                        pltpu.SemaphoreType.DMA((2, 2))],
    )
    return pl.pallas_call(
        functools.partial(_mla_sample_kernel, layer=layer, n_pages=n_pages, n_seq=n_seq),
        grid_spec=grid_spec,
        out_shape=jax.ShapeDtypeStruct((n_seq, MLA_HEADS, KV_LORA), BF16),
        compiler_params=_cp(("arbitrary",)),
        name="mla_sample",
    )(page_table_flat, q, k_new, cache_ckv, cache_krope_t)


def _cross_kernel(q_ref, mk_ref, mv_ref, o_ref):
    scale = MEM_HEAD_DIM ** -0.5
    for h in range(MEM_HEADS):
        hs = slice(h * MEM_HEAD_DIM, (h + 1) * MEM_HEAD_DIM)
        q = q_ref[0, :, hs]
        kh = _bf(mk_ref[0, :, hs])
        vh = _bf(mv_ref[0, :, hs])
        s = _dot_nt(q, kh) * scale
        m = jnp.max(s, axis=1, keepdims=True)
        p = jnp.exp(s - m)
        l = jnp.sum(p, axis=1, keepdims=True)
        o = _dot(_bf(p), vh) / l
        o_ref[0, :, hs] = o.astype(o_ref.dtype)


def cross_attend(q, mem_k, mem_v, tq):
    b, t, d = q.shape
    return pl.pallas_call(
        _cross_kernel,
        grid=(b, t // tq),
        in_specs=[pl.BlockSpec((1, tq, d), lambda i, j: (i, j, 0)),
                  pl.BlockSpec((1, N_MEM, d), lambda i, j: (i, 0, 0)),
                  pl.BlockSpec((1, N_MEM, d), lambda i, j: (i, 0, 0))],
        out_specs=pl.BlockSpec((1, tq, d), lambda i, j: (i, j, 0)),
        out_shape=jax.ShapeDtypeStruct((b, t, d), BF16),
        compiler_params=_cp(("parallel", "arbitrary")),
        name="cross_attend",
    )(q, mem_k, mem_v)


def _cross_sample_kernel(q_ref, mk_ref, mv_ref, o_ref):
    scale = MEM_HEAD_DIM ** -0.5
    q = q_ref[0].astype(F32)
    q4 = jnp.concatenate([q[:, h * MEM_HEAD_DIM:(h + 1) * MEM_HEAD_DIM] for h in range(MEM_HEADS)], axis=0)
    s = jnp.sum(mk_ref[0, 0] * q4[None], axis=2, keepdims=True) * scale
    m = jnp.max(s, axis=0, keepdims=True)
    p = jnp.exp(s - m)
    l = jnp.sum(p, axis=0)
    o = jnp.sum(p * mv_ref[0, 0], axis=0) / l
    for h in range(MEM_HEADS):
        o_ref[0, :, h * MEM_HEAD_DIM:(h + 1) * MEM_HEAD_DIM] = o[h:h + 1].astype(o_ref.dtype)


def cross_sample(q, mem_k, mem_v, layer):
    b, _, d = q.shape
    blk = pl.BlockSpec((1, 1, N_MEM, MEM_HEADS, MEM_HEAD_DIM), lambda i: (layer, i, 0, 0, 0))
    return pl.pallas_call(
        _cross_sample_kernel,
        grid=(b,),
        in_specs=[pl.BlockSpec((1, 1, d), lambda i: (i, 0, 0)), blk, blk],
        out_specs=pl.BlockSpec((1, 1, d), lambda i: (i, 0, 0)),
        out_shape=jax.ShapeDtypeStruct((b, 1, d), BF16),
        compiler_params=_cp(("parallel",)),
        name="cross_sample",
    )(q, mem_k, mem_v)


def _rot_cols(w):
    half = QK_ROPE_DIM // 2
    return jnp.concatenate([-w[..., half:], w[..., :half]], axis=-1)


def _pad_lanes(w):
    return jnp.pad(w, [(0, 0)] * (w.ndim - 1) + [(0, LANES - w.shape[-1])])


def _layer_weights(l, w_in, shift_mu, rwkv_w0, rwkv_w_up, rwkv_a0, rwkv_a_up, rwkv_g_up, rwkv_k_k, rwkv_k_a,
                   rwkv_r_k, rwkv_gn_w, rwkv_gn_b, mla_q_norm, mla_w_uq, mla_kv_norm, mla_w_uk, mla_w_uv):
    w3 = RWKV_WIDTH
    o_kr = RWKV_PROJ + Q_LORA + KV_LORA
    w_kr = w_in[l][:, o_kr:]
    w_in_ext = _bf(jnp.concatenate([w_in[l][:, :o_kr], _pad_lanes(w_kr), _pad_lanes(_rot_cols(w_kr))], axis=1))
    w_lora = jnp.zeros((LORA_SLAB, 3 * w3), F32)
    w_lora = w_lora.at[:DECAY_LORA, :w3].set(rwkv_w_up[l])
    w_lora = w_lora.at[DECAY_LORA:DECAY_LORA + ICLR_LORA, w3:2 * w3].set(rwkv_a_up[l])
    w_lora = w_lora.at[DECAY_LORA + ICLR_LORA:, 2 * w3:].set(rwkv_g_up[l])
    uq = mla_w_uq[l].reshape(Q_LORA, MLA_HEADS, QK_NOPE_DIM + QK_ROPE_DIM)
    uq_nope = uq[:, :, :QK_NOPE_DIM].reshape(Q_LORA, MLA_HEADS * QK_NOPE_DIM)
    uq_rope = uq[:, :, QK_NOPE_DIM:]
    w_q = _bf(jnp.concatenate([uq_nope,
                               _pad_lanes(uq_rope).reshape(Q_LORA, MLA_HEADS * LANES),
                               _pad_lanes(_rot_cols(uq_rope)).reshape(Q_LORA, MLA_HEADS * LANES)], axis=1))
    eye_h = jnp.eye(MLA_HEADS, dtype=F32)
    wuk_bd = _bf(jnp.einsum('chd,hg->hdgc', mla_w_uk[l], eye_h).reshape(MLA_HEADS * QK_NOPE_DIM, MLA_HEADS * KV_LORA))
    wuv_bd = _bf(jnp.einsum('chd,hg->hcgd', mla_w_uv[l], eye_h).reshape(MLA_HEADS * KV_LORA, MLA_HEADS * V_HEAD_DIM))
    row = lambda a: a.reshape(1, -1)
    return dict(
        w_in_ext=w_in_ext, mu=row(shift_mu[l]), w0=row(rwkv_w0[l]), a0=row(rwkv_a0[l]), w_lora=w_lora,
        k_k=row(rwkv_k_k[l]), k_a=row(rwkv_k_a[l]), rk=row(rwkv_r_k[l]), gnw=row(rwkv_gn_w[l]), gnb=row(rwkv_gn_b[l]),
        q_norm=row(mla_q_norm[l]), kv_norm=row(mla_kv_norm[l]), w_q=w_q, wuk_bd=wuk_bd, wuv_bd=wuv_bd)


def _rope_tables(pos):
    half = QK_ROPE_DIM // 2
    inv_freq = ROPE_THETA ** (-jnp.arange(half, dtype=F32) / half)
    ang = pos.astype(F32)[:, None] * inv_freq[None, :]
    cos, sin = jnp.cos(ang), jnp.sin(ang)
    return (_pad_lanes(jnp.concatenate([cos, cos], axis=1)), _pad_lanes(jnp.concatenate([sin, sin], axis=1)))


def kernel(x_prompt, x_sample, cache_ckv, cache_krope, state_wkv, state_shift, cache_mem_k, cache_mem_v, page_table, mem_prompt, norm_ffn1, ffn1_w_gate, ffn1_w_up, ffn1_w_down, norm_mix, w_in, shift_mu, rwkv_w0, rwkv_w_up, rwkv_a0, rwkv_a_up, rwkv_g_up, rwkv_k_k, rwkv_k_a, rwkv_r_k, rwkv_gn_w, rwkv_gn_b, mla_q_norm, mla_w_uq, mla_kv_norm, mla_w_uk, mla_w_uv, w_out, norm_cross, norm_mem, mem_w_q, mem_w_k, mem_w_v, mem_w_o, norm_ffn2, ffn2_w_gate, ffn2_w_up, ffn2_w_down, norm_final):
    bp, tp, d = x_prompt.shape
    bs, ts, _ = x_sample.shape
    assert ts == 1
    n_pages = page_table.shape[1]
    assert n_pages % PAGE_GROUP == 0
    past_len = n_pages * PAGE_SIZE
    hh, nn = RWKV_HEADS, RWKV_HEAD_DIM
    row = lambda a: a.reshape(1, -1)

    head_of = jnp.arange(RWKV_WIDTH) // nn
    ones_bd = (head_of[:, None] == head_of[None, :]).astype(F32)
    cos_p, sin_p = _rope_tables(jnp.arange(tp))
    cos_s, sin_s = _rope_tables(jnp.full((bs,), past_len, jnp.int32))
    pt_flat = page_table.reshape(-1)
    krope_t = jnp.swapaxes(cache_krope, 2, 3)

    lw = [_layer_weights(l, w_in, shift_mu, rwkv_w0, rwkv_w_up, rwkv_a0, rwkv_a_up, rwkv_g_up, rwkv_k_k, rwkv_k_a,
                         rwkv_r_k, rwkv_gn_w, rwkv_gn_b, mla_q_norm, mla_w_uq, mla_kv_norm, mla_w_uk, mla_w_uv)
          for l in range(DEPTH)]

    tm_p, tm_s = 512, bs
    tf = D_FF // 2

    def run_ffn(x2, l, which, tm, final):
        if which == 1:
            g, wg, wu, wd = norm_ffn1[l], ffn1_w_gate[l], ffn1_w_up[l], ffn1_w_down[l]
        else:
            g, wg, wu, wd = norm_ffn2[l], ffn2_w_gate[l], ffn2_w_up[l], ffn2_w_down[l]
        return ffn(x2, row(g), _bf(wg), _bf(wu), _bf(wd), row(norm_final), final, tm, tf)

    mem2 = mem_prompt.reshape(bp * N_MEM, d)
    mem_k_p, mem_v_p = [], []
    for l in range(DEPTH):
        kv = norm_matmul(mem2, row(norm_mem[l]), _bf(jnp.concatenate([mem_w_k[l], mem_w_v[l]], axis=1)), F32, 512)
        mem_k_p.append(kv[:, :d].reshape(bp, N_MEM, d))
        mem_v_p.append(kv[:, d:].reshape(bp, N_MEM, d))

    def trunk(x, sample):
        b, t, _ = x.shape
        m = b * t
        tm = tm_s if sample else tm_p
        x2 = x.reshape(m, d)
        ckvs, krs, wkvs, shifts = [], [], [], []
        for l in range(DEPTH):
            p = lw[l]
            x2 = run_ffn(x2, l, 1, tm, False)
            proj = norm_matmul(x2, row(norm_mix[l]), p['w_in_ext'], F32, tm)
            if sample:
                outs = prep(proj.reshape(1, m, IN_PROJ_EXT), state_shift[l].reshape(1, m, RWKV_PROJ), True,
                            p['mu'], p['w0'], p['a0'], p['w_lora'], p['k_k'], p['k_a'], ones_bd, p['q_norm'],
                            p['kv_norm'], p['w_q'], p['wuk_bd'], cos_s, sin_s, m)
                shift_last = proj[:, :RWKV_PROJ]
            else:
                outs = prep(proj.reshape(b, t, IN_PROJ_EXT), jnp.zeros((b, 1, RWKV_PROJ), F32), False,
                            p['mu'], p['w0'], p['a0'], p['w_lora'], p['k_k'], p['k_a'], ones_bd, p['q_norm'],
                            p['kv_norm'], p['w_q'], p['wuk_bd'], cos_p, sin_p, 256)
                shift_last = outs[11].reshape(b, RWKV_PROJ)
            r, k, v, kk, bv, lwd, g, ckv, kr, q_cat, k_cat = outs[:11]
            if sample:
                to_heads = lambda a: a.reshape(m, hh, nn)
                o_rwkv, wkv = rwkv_step(to_heads(r), to_heads(k), to_heads(v), to_heads(kk), to_heads(bv),
                                        to_heads(lwd), to_heads(g), p['rk'].reshape(hh, nn), p['gnw'].reshape(hh, nn),
                                        p['gnb'].reshape(hh, nn), state_wkv[l], 8)
                o_rwkv = _bf(o_rwkv.reshape(m, RWKV_WIDTH))
                o_lat = mla_sample(pt_flat, q_cat.reshape(m, MLA_HEADS, QCAT), k_cat.reshape(m, 1, QCAT),
                                   cache_ckv, krope_t, l, n_pages).reshape(m, MLA_HEADS * KV_LORA)
            else:
                o_rwkv, s_bd = rwkv_chunked(r, k, v, kk, bv, lwd, g, p['rk'], p['gnw'], p['gnb'], CHUNK)
                o_rwkv = o_rwkv.reshape(m, RWKV_WIDTH)
                wkv = jnp.stack([s_bd[:, :, :nn, :nn], s_bd[:, :, nn:, nn:]], axis=2).reshape(b, hh, nn, nn)
                o_lat = mla_prompt(q_cat, k_cat, 256, 256).reshape(m, MLA_HEADS * KV_LORA)
            x2 = mix_out(o_rwkv, o_lat, p['wuv_bd'], _bf(w_out[l]), x2, tm)
            if sample:
                q = norm_matmul(x2, row(norm_cross[l]), _bf(mem_w_q[l]), F32, tm)
                att = cross_sample(q.reshape(b, t, d), cache_mem_k, cache_mem_v, l)
            else:
                q = norm_matmul(x2, row(norm_cross[l]), _bf(mem_w_q[l]), BF16, tm)
                att = cross_attend(q.reshape(b, t, d), mem_k_p[l], mem_v_p[l], 512)
            x2 = matmul_res(att.reshape(m, d), _bf(mem_w_o[l]), x2, tm)
            x2 = run_ffn(x2, l, 2, tm, l == DEPTH - 1)
            ckvs.append(ckv.reshape(b, t, KV_LORA))
            krs.append(kr.reshape(b, t, QK_ROPE_DIM))
            wkvs.append(wkv)
            shifts.append(shift_last)
        return x2.reshape(b, t, d), jnp.stack(ckvs), jnp.stack(krs), jnp.stack(wkvs), jnp.stack(shifts)

    y_p, ckv_p, kr_p, wkv_p, shift_p = trunk(x_prompt, False)
    y_s, ckv_s, kr_s, wkv_s, shift_s = trunk(x_sample, True)
    mem_shape = (DEPTH, bp, N_MEM, MEM_HEADS, MEM_HEAD_DIM)
    return (y_p, y_s, ckv_p, kr_p, wkv_p, shift_p,
            jnp.stack(mem_k_p).reshape(mem_shape), jnp.stack(mem_v_p).reshape(mem_shape),
            ckv_s, kr_s, wkv_s, shift_s)
```

```python
import functools

import jax
import jax.numpy as jnp
from jax import lax
from jax.experimental import pallas as pl
from jax.experimental.pallas import tpu as pltpu

F32 = jnp.float32
BF16 = jnp.bfloat16

D_MODEL = 1024
DEPTH = 2
PAGE_SIZE = 128
RWKV_HEADS = 8
RWKV_HEAD_DIM = 64
RWKV_WIDTH = 512
DECAY_LORA = 64
ICLR_LORA = 64
GATE_LORA = 128
GN_EPS = 64e-5
MLA_HEADS = 8
QK_NOPE_DIM = 64
QK_ROPE_DIM = 32
V_HEAD_DIM = 64
Q_LORA = 256
KV_LORA = 128
ROPE_THETA = 10000.0
MLA_SCALE = (QK_NOPE_DIM + QK_ROPE_DIM) ** -0.5
RWKV_PROJ = 3 * RWKV_WIDTH + DECAY_LORA + ICLR_LORA + GATE_LORA
N_MEM = 256
MEM_HEADS = 4
MEM_HEAD_DIM = D_MODEL // MEM_HEADS
D_FF = 2816
RMS_EPS = 1e-6

LANES = 128
IN_PROJ_EXT = RWKV_PROJ + Q_LORA + KV_LORA + 2 * LANES
LORA_SLAB = DECAY_LORA + ICLR_LORA + GATE_LORA
QCAT = 2 * LANES
CHUNK = 64
CHUNK_SEQS = 4
PAGE_GROUP = 32
NEG = -1e30
VMEM_LIMIT = 56 * 1024 * 1024


def _cp(sem):
    return pltpu.CompilerParams(dimension_semantics=sem, vmem_limit_bytes=VMEM_LIMIT)


def _dot(a, b):
    return jnp.dot(a, b, preferred_element_type=F32)


def _dot_nt(a, b):
    return lax.dot_general(a, b, (((1,), (1,)), ((), ())), preferred_element_type=F32)


def _bf(x):
    return x.astype(BF16)


def _split3(x):
    hi = _bf(x)
    r1 = x - hi.astype(F32)
    mid = _bf(r1)
    return hi, mid, _bf(r1 - mid.astype(F32))


def _dot_split2(x, w):
    hi = _bf(x)
    return _dot(hi, w) + _dot(_bf(x - hi.astype(F32)), w)


def _rms(x, g):
    return x * lax.rsqrt(jnp.mean(x * x, axis=-1, keepdims=True) + RMS_EPS) * g


def _sigmoid(x):
    return 1.0 / (1.0 + jnp.exp(-x))


def _norm_matmul_kernel(x_ref, g_ref, w_ref, o_ref):
    h = _bf(_rms(x_ref[...], g_ref[...]))
    o_ref[...] = _dot(h, w_ref[...]).astype(o_ref.dtype)


def norm_matmul(x, g, w, out_dtype, tm):
    m, d = x.shape
    n = w.shape[1]
    return pl.pallas_call(
        _norm_matmul_kernel,
        grid=(m // tm,),
        in_specs=[pl.BlockSpec((tm, d), lambda i: (i, 0)),
                  pl.BlockSpec((1, d), lambda i: (0, 0)),
                  pl.BlockSpec((d, n), lambda i: (0, 0))],
        out_specs=pl.BlockSpec((tm, n), lambda i: (i, 0)),
        out_shape=jax.ShapeDtypeStruct((m, n), out_dtype),
        compiler_params=_cp(("parallel",)),
        name="norm_matmul",
    )(x, g, w)


def _ffn_kernel(x_ref, g_ref, wg_ref, wu_ref, wd_ref, gf_ref, o_ref, h_scr, acc_scr, *, nf, final_norm):
    f = pl.program_id(1)

    @pl.when(f == 0)
    def _():
        h_scr[...] = _bf(_rms(x_ref[...], g_ref[...]))
        acc_scr[...] = jnp.zeros_like(acc_scr)

    h = h_scr[...]
    gate = _dot(h, wg_ref[...])
    up = _dot(h, wu_ref[...])
    act = _bf(gate * _sigmoid(gate) * up)
    acc_scr[...] += _dot(act, wd_ref[...])

    @pl.when(f == nf - 1)
    def _():
        y = x_ref[...] + 0.5 * acc_scr[...]
        if final_norm:
            y = _rms(y, gf_ref[...])
        o_ref[...] = y


def ffn(x, g, wg, wu, wd, gf, final_norm, tm, tf):
    m, d = x.shape
    dff = wg.shape[1]
    nf = dff // tf
    return pl.pallas_call(
        functools.partial(_ffn_kernel, nf=nf, final_norm=final_norm),
        grid=(m // tm, nf),
        in_specs=[pl.BlockSpec((tm, d), lambda i, f: (i, 0)),
                  pl.BlockSpec((1, d), lambda i, f: (0, 0)),
                  pl.BlockSpec((d, tf), lambda i, f: (0, f)),
                  pl.BlockSpec((d, tf), lambda i, f: (0, f)),
                  pl.BlockSpec((tf, d), lambda i, f: (f, 0)),
                  pl.BlockSpec((1, d), lambda i, f: (0, 0))],
        out_specs=pl.BlockSpec((tm, d), lambda i, f: (i, 0)),
        out_shape=jax.ShapeDtypeStruct((m, d), F32),
        scratch_shapes=[pltpu.VMEM((tm, d), BF16), pltpu.VMEM((tm, d), F32)],
        compiler_params=_cp(("parallel", "arbitrary")),
        name="ffn",
    )(x, g, wg, wu, wd, gf)


def _matmul_res_kernel(a_ref, w_ref, x_ref, o_ref):
    o_ref[...] = x_ref[...] + _dot(a_ref[...], w_ref[...])


def matmul_res(a, w, x, tm):
    m, k = a.shape
    n = w.shape[1]
    return pl.pallas_call(
        _matmul_res_kernel,
        grid=(m // tm,),
        in_specs=[pl.BlockSpec((tm, k), lambda i: (i, 0)),
                  pl.BlockSpec((k, n), lambda i: (0, 0)),
                  pl.BlockSpec((tm, n), lambda i: (i, 0))],
        out_specs=pl.BlockSpec((tm, n), lambda i: (i, 0)),
        out_shape=jax.ShapeDtypeStruct((m, n), F32),
        compiler_params=_cp(("parallel",)),
        name="matmul_res",
    )(a, w, x)


def _mix_out_kernel(orw_ref, olat_ref, wuv_ref, wout_ref, x_ref, o_ref):
    o_mla = _bf(_dot(olat_ref[...], wuv_ref[...]))
    y = _dot(orw_ref[...], wout_ref[:RWKV_WIDTH, :]) + _dot(o_mla, wout_ref[RWKV_WIDTH:, :])
    o_ref[...] = x_ref[...] + y


def mix_out(o_rwkv, o_lat, wuv_bd, w_out, x, tm):
    m, d = x.shape
    return pl.pallas_call(
        _mix_out_kernel,
        grid=(m // tm,),
        in_specs=[pl.BlockSpec((tm, RWKV_WIDTH), lambda i: (i, 0)),
                  pl.BlockSpec((tm, MLA_HEADS * KV_LORA), lambda i: (i, 0)),
                  pl.BlockSpec(wuv_bd.shape, lambda i: (0, 0)),
                  pl.BlockSpec(w_out.shape, lambda i: (0, 0)),
                  pl.BlockSpec((tm, d), lambda i: (i, 0))],
        out_specs=pl.BlockSpec((tm, d), lambda i: (i, 0)),
        out_shape=jax.ShapeDtypeStruct((m, d), F32),
        compiler_params=_cp(("parallel",)),
        name="mix_out",
    )(o_rwkv, o_lat, wuv_bd, w_out, x)


def _prep_kernel(proj_ref, prev_ref, mu_ref, w0_ref, a0_ref, wl_ref, kk_ref, ka_ref, ones_ref,
                 qn_ref, kvn_ref, wq_ref, wuk_ref, cos_ref, sin_ref,
                 r_o, k_o, v_o, kkn_o, b_o, lw_o, g_o, ckv_o, kr_o, qcat_o, kcat_o, last_o,
                 carry, *, tm, explicit_prev):
    p = proj_ref[0, :, :RWKV_PROJ]
    if explicit_prev:
        prev = prev_ref[0]
    else:
        @pl.when(pl.program_id(1) == 0)
        def _():
            carry[...] = prev_ref[0]

        row = lax.broadcasted_iota(jnp.int32, (tm, 1), 0)
        prev = jnp.where(row == 0, carry[...], pltpu.roll(p, 1, axis=0))
        carry[...] = p[tm - 1:tm, :]
    last_o[0] = p[tm - 1:tm, :]
    ps = p + (prev - p) * mu_ref[...]

    w3 = RWKV_WIDTH
    r = ps[:, :w3]
    k = ps[:, w3:2 * w3]
    v = ps[:, 2 * w3:3 * w3]
    slab = ps[:, 3 * w3:]
    lane = lax.broadcasted_iota(jnp.int32, slab.shape, 1)
    act = jnp.where(lane < DECAY_LORA, jnp.tanh(slab),
                    jnp.where(lane < DECAY_LORA + ICLR_LORA, slab, _sigmoid(slab)))
    lo = _dot_split2(act, wl_ref[...])
    z = -(w0_ref[...] + lo[:, :w3])
    softplus = jnp.maximum(z, 0.0) + jnp.log(1.0 + jnp.exp(-jnp.abs(z)))
    lw = -jnp.exp(-softplus - 0.5)
    a = _sigmoid(a0_ref[...] + lo[:, w3:2 * w3])
    g = lo[:, 2 * w3:]
    kkr = k * kk_ref[...]
    ssq = _dot_split2(kkr * kkr, ones_ref[...])
    kkn = kkr / jnp.maximum(jnp.sqrt(ssq), 1e-12)
    r_o[0] = r
    k_o[0] = k * (1.0 + (a - 1.0) * ka_ref[...])
    v_o[0] = v
    kkn_o[0] = kkn
    b_o[0] = kkn * a
    lw_o[0] = lw
    g_o[0] = g

    o1 = RWKV_PROJ
    cq = _bf(_rms(proj_ref[0, :, o1:o1 + Q_LORA], qn_ref[...]))
    o2 = o1 + Q_LORA
    ckv = _rms(proj_ref[0, :, o2:o2 + KV_LORA], kvn_ref[...])
    o3 = o2 + KV_LORA
    cos = cos_ref[...]
    sin = sin_ref[...]
    kr = proj_ref[0, :, o3:o3 + LANES] * cos + proj_ref[0, :, o3 + LANES:o3 + 2 * LANES] * sin
    ckv_o[0] = ckv
    kr_o[0] = kr[:, :QK_ROPE_DIM]
    kcat_o[0] = _bf(jnp.concatenate([ckv, kr], axis=1))

    qq = _dot(cq, wq_ref[...])
    nn = MLA_HEADS * QK_NOPE_DIM
    nr = MLA_HEADS * LANES
    q_lat = _dot(_bf(qq[:, :nn]), wuk_ref[...]) * MLA_SCALE
    cos8 = jnp.concatenate([cos] * MLA_HEADS, axis=1)
    sin8 = jnp.concatenate([sin] * MLA_HEADS, axis=1)
    q_rp = (qq[:, nn:nn + nr] * cos8 + qq[:, nn + nr:] * sin8) * MLA_SCALE
    pieces = []
    for h in range(MLA_HEADS):
        pieces.append(q_lat[:, h * LANES:(h + 1) * LANES])
        pieces.append(q_rp[:, h * LANES:(h + 1) * LANES])
    qcat_o[0] = _bf(jnp.concatenate(pieces, axis=1))


def prep(proj, prev, explicit_prev, mu, w0, a0, w_lora, k_k, k_a, ones_bd, q_norm, kv_norm, w_q, wuk_bd, cos_t, sin_t, tm):
    b, t, n = proj.shape
    nt = t // tm
    wide = lambda w: pl.BlockSpec((1, tm, w), lambda i, j: (i, j, 0))
    full = lambda arr: pl.BlockSpec(arr.shape, lambda i, j: (0,) * arr.ndim)
    if explicit_prev:
        prev_spec = pl.BlockSpec((1, tm, RWKV_PROJ), lambda i, j: (i, j, 0))
    else:
        prev_spec = pl.BlockSpec((1, 1, RWKV_PROJ), lambda i, j: (i, 0, 0))
    sd = lambda w, dt: jax.ShapeDtypeStruct((b, t, w), dt)
    w3 = RWKV_WIDTH
    out_shape = [sd(w3, F32)] * 7 + [sd(KV_LORA, F32), sd(QK_ROPE_DIM, F32),
                                      sd(MLA_HEADS * QCAT, BF16), sd(QCAT, BF16),
                                      jax.ShapeDtypeStruct((b, 1, RWKV_PROJ), F32)]
    out_specs = [wide(w3)] * 7 + [wide(KV_LORA), wide(QK_ROPE_DIM), wide(MLA_HEADS * QCAT), wide(QCAT),
                                   pl.BlockSpec((1, 1, RWKV_PROJ), lambda i, j: (i, 0, 0))]
    return pl.pallas_call(
        functools.partial(_prep_kernel, tm=tm, explicit_prev=explicit_prev),
        grid=(b, nt),
        in_specs=[wide(n), prev_spec, full(mu), full(w0), full(a0), full(w_lora), full(k_k), full(k_a),
                  full(ones_bd), full(q_norm), full(kv_norm), full(w_q), full(wuk_bd),
                  pl.BlockSpec((tm, LANES), lambda i, j: (j, 0)),
                  pl.BlockSpec((tm, LANES), lambda i, j: (j, 0))],
        out_specs=out_specs,
        out_shape=out_shape,
        scratch_shapes=[pltpu.VMEM((1, RWKV_PROJ), F32)],
        compiler_params=_cp(("arbitrary", "arbitrary")),
        name="prep",
    )(proj, prev, mu, w0, a0, w_lora, k_k, k_a, ones_bd, q_norm, kv_norm, w_q, wuk_bd, cos_t, sin_t)


def _rwkv_chunk_kernel(r_ref, k_ref, v_ref, kk_ref, b_ref, lw_ref, g_ref, rk_ref, gnw_ref, gnb_ref,
                       o_ref, s_out_ref, s_scr, *, c, n_chunks, nb):
    @pl.when(pl.program_id(1) == 0)
    def _():
        s_scr[...] = jnp.zeros_like(s_scr)

    n = RWKV_HEAD_DIM
    npair = RWKV_HEADS // 2
    row = lax.broadcasted_iota(jnp.int32, (c, c), 0)
    col = lax.broadcasted_iota(jnp.int32, (c, c), 1)
    incl = row >= col
    strict = row > col
    tri = _bf(jnp.where(incl, 1.0, 0.0))
    eye = jnp.where(row == col, 1.0, 0.0).astype(F32)
    lane = lax.broadcasted_iota(jnp.int32, (c, LANES), 1)
    is_e = lane < n
    r2 = lax.broadcasted_iota(jnp.int32, (LANES, LANES), 0)
    c2 = lax.broadcasted_iota(jnp.int32, (LANES, LANES), 1)
    same_head = (r2 < n) == (c2 < n)
    ones_bd = _bf(jnp.where(same_head, 1.0, 0.0))
    n_double = c.bit_length() - 2
    pairs = range(nb * npair)
    heads = [(j, x) for j in pairs for x in range(2)]
    sl = [slice((j % npair) * LANES, (j % npair + 1) * LANES) for j in pairs]
    bi = [j // npair for j in pairs]

    r = [r_ref[bi[j], :, sl[j]] for j in pairs]
    k = [k_ref[bi[j], :, sl[j]] for j in pairs]
    v = [v_ref[bi[j], :, sl[j]] for j in pairs]
    kk = [kk_ref[bi[j], :, sl[j]] for j in pairs]
    bv = [b_ref[bi[j], :, sl[j]] for j in pairs]
    lw = [lw_ref[bi[j], :, sl[j]] for j in pairs]
    s0 = [s_scr[bi[j], j % npair] for j in pairs]

    cum = [sum(_dot(tri, part) for part in _split3(lw[j])) for j in pairs]
    e_pos = [jnp.exp(cum[j]) for j in pairs]
    e_neg = [jnp.exp(-cum[j]) for j in pairs]
    a_t = [-kk[j] * jnp.exp(cum[j] - lw[j]) for j in pairs]
    r_t = [r[j] * e_pos[j] for j in pairs]
    b_t = [bv[j] * e_neg[j] for j in pairs]
    k_t = [k[j] * e_neg[j] for j in pairs]
    g_c = [e_pos[j][c - 1:c, :] for j in pairs]

    v_b = [_bf(v[j]) for j in pairs]
    s0_b = [_bf(s0[j]) for j in pairs]
    zero = jnp.zeros((c, LANES), F32)
    xb, xk = [], []
    for j in pairs:
        lhs = _bf(jnp.concatenate([jnp.where(is_e, a_t[j], zero), jnp.where(is_e, r_t[j], zero),
                                   jnp.where(is_e, zero, a_t[j]), jnp.where(is_e, zero, r_t[j])], axis=0))
        xb.append(_dot_nt(lhs, _bf(b_t[j])))
        xk.append(_dot_nt(lhs, _bf(k_t[j])))
    rs0 = [_dot_nt(_bf(r_t[j]), s0_b[j]) for j in pairs]

    nmat, arb, aakv, o2 = {}, {}, {}, {}
    for (j, x) in heads:
        base = 2 * c * x
        nmat[j, x] = jnp.where(strict, xb[j][base:base + c], 0.0)
        arb[j, x] = _bf(jnp.where(incl, xb[j][base + c:base + 2 * c], 0.0))
        aakv[j, x] = _dot(_bf(jnp.where(strict, xk[j][base:base + c], 0.0)), v_b[j])
        o2[j, x] = _dot(_bf(jnp.where(incl, xk[j][base + c:base + 2 * c], 0.0)), v_b[j])

    tinv = {h: eye + nmat[h] for h in heads}
    pw = dict(nmat)
    for _ in range(n_double):
        pw = {h: _dot(_bf(pw[h]), _bf(pw[h])) for h in heads}
        tinv = {h: tinv[h] + _dot(_bf(pw[h]), _bf(tinv[h])) for h in heads}

    w12 = {h: _dot(_bf(tinv[h]), _bf(jnp.concatenate([a_t[h[0]], aakv[h]], axis=1))) for h in heads}
    u = []
    for j in pairs:
        w1 = jnp.where(is_e, w12[j, 0][:, :LANES], w12[j, 1][:, :LANES])
        w2 = jnp.where(is_e, w12[j, 0][:, LANES:], w12[j, 1][:, LANES:])
        u.append(_dot_nt(_bf(w1), s0_b[j]) + w2)
    u_b = [_bf(u[j]) for j in pairs]
    o = [rs0[j] + jnp.where(is_e, _dot(arb[j, 0], u_b[j]) + o2[j, 0], _dot(arb[j, 1], u_b[j]) + o2[j, 1])
         for j in pairs]
    for j in pairs:
        uv = jnp.concatenate([u[j], v[j]], axis=0)
        bk = jnp.concatenate([b_t[j] * g_c[j], k_t[j] * g_c[j]], axis=0)
        s_scr[bi[j], j % npair] = s0[j] * g_c[j] + jnp.where(same_head, _dot(_bf(uv.T), _bf(bk)), 0.0)

    inv_n = 1.0 / n
    o_all = jnp.concatenate(o, axis=0)
    mean = _dot(_bf(o_all), ones_bd) * inv_n
    d = o_all - mean
    var = _dot(_bf(d * d), ones_bd) * inv_n
    bsum = _dot(_bf(jnp.concatenate([r[j] * k[j] * rk_ref[:, sl[j]] for j in pairs], axis=0)), ones_bd)
    on = d * lax.rsqrt(var + GN_EPS)
    for j in pairs:
        rows = slice(j * c, (j + 1) * c)
        out = (on[rows] * gnw_ref[:, sl[j]] + gnb_ref[:, sl[j]] + bsum[rows] * v[j]) * g_ref[bi[j], :, sl[j]]
        o_ref[bi[j], :, sl[j]] = out.astype(o_ref.dtype)

    @pl.when(pl.program_id(1) == n_chunks - 1)
    def _():
        s_out_ref[...] = s_scr[...]


def rwkv_chunked(r, k, v, kk, bv, lw, g, rk, gnw, gnb, c, nb):
    b, t, w = r.shape
    n_chunks = t // c
    seq = pl.BlockSpec((nb, c, w), lambda i, j: (i, j, 0))
    par = pl.BlockSpec((1, w), lambda i, j: (0, 0))
    npair = RWKV_HEADS // 2
    return pl.pallas_call(
        functools.partial(_rwkv_chunk_kernel, c=c, n_chunks=n_chunks, nb=nb),
        grid=(b // nb, n_chunks),
        in_specs=[seq] * 7 + [par] * 3,
        out_specs=[seq, pl.BlockSpec((nb, npair, LANES, LANES), lambda i, j: (i, 0, 0, 0))],
        out_shape=[jax.ShapeDtypeStruct((b, t, w), BF16),
                   jax.ShapeDtypeStruct((b, npair, LANES, LANES), F32)],
        scratch_shapes=[pltpu.VMEM((nb, npair, LANES, LANES), F32)],
        compiler_params=_cp(("arbitrary", "arbitrary")),
        name="rwkv_chunked",
    )(r, k, v, kk, bv, lw, g, rk, gnw, gnb)


def _rwkv_step_kernel(r_ref, k_ref, v_ref, kk_ref, b_ref, lw_ref, g_ref, rk_ref, gnw_ref, gnb_ref, s_ref,
                      o_ref, s_out_ref, *, bb):
    n = RWKV_HEAD_DIM
    eye = lax.broadcasted_iota(jnp.int32, (n, n), 0) == lax.broadcasted_iota(jnp.int32, (n, n), 1)

    def body(i, carry):
        for h in range(RWKV_HEADS):
            hs = slice(h, h + 1)
            s = s_ref[i, h]
            r = r_ref[i, hs, :]
            k = k_ref[i, hs, :]
            v = v_ref[i, hs, :]
            kk = kk_ref[i, hs, :]
            bv = b_ref[i, hs, :]
            w = jnp.exp(lw_ref[i, hs, :])
            s_kk = jnp.sum(s * kk, axis=1, keepdims=True)
            v_col = jnp.sum(jnp.where(eye, v, 0.0), axis=1, keepdims=True)
            s_new = s * w - s_kk * bv + v_col * k
            o_col = jnp.sum(s_new * r, axis=1, keepdims=True)
            o = jnp.sum(jnp.where(eye, o_col, 0.0), axis=0, keepdims=True)
            mean = jnp.mean(o, axis=1, keepdims=True)
            d = o - mean
            var = jnp.mean(d * d, axis=1, keepdims=True)
            on = d * lax.rsqrt(var + GN_EPS) * gnw_ref[hs, :] + gnb_ref[hs, :]
            bonus = jnp.sum(r * k * rk_ref[hs, :], axis=1, keepdims=True) * v
            o_ref[i, hs, :] = ((on + bonus) * g_ref[i, hs, :]).astype(o_ref.dtype)
            s_out_ref[i, h] = s_new
        return carry

    lax.fori_loop(0, bb, body, 0)


def rwkv_step(r, k, v, kk, bv, lw, g, rk, gnw, gnb, s, bb):
    b = r.shape[0]
    h, n = RWKV_HEADS, RWKV_HEAD_DIM
    vec = pl.BlockSpec((bb, h, n), lambda i: (i, 0, 0))
    par = pl.BlockSpec((h, n), lambda i: (0, 0))
    st = pl.BlockSpec((bb, h, n, n), lambda i: (i, 0, 0, 0))
    return pl.pallas_call(
        functools.partial(_rwkv_step_kernel, bb=bb),
        grid=(b // bb,),
        in_specs=[vec] * 7 + [par] * 3 + [st],
        out_specs=[vec, st],
        out_shape=[jax.ShapeDtypeStruct((b, h, n), F32), jax.ShapeDtypeStruct((b, h, n, n), F32)],
        compiler_params=_cp(("parallel",)),
        name="rwkv_step",
    )(r, k, v, kk, bv, lw, g, rk, gnw, gnb, s)


def _mla_prompt_kernel(q_ref, k_ref, o_ref, m_scr, l_scr, acc_scr, *, t):
    i = pl.program_id(1)
    nh = MLA_HEADS
    q = jnp.concatenate([q_ref[0, :, h * QCAT:(h + 1) * QCAT] for h in range(nh)], axis=0)
    row = lax.broadcasted_iota(jnp.int32, (t, 1), 0)
    causal = lax.broadcasted_iota(jnp.int32, (1, t), 1) <= jnp.concatenate([row] * nh, axis=0)
    ones = jnp.ones((t, LANES), BF16)
    m_scr[...] = jnp.full_like(m_scr, NEG)
    l_scr[...] = jnp.zeros_like(l_scr)
    acc_scr[...] = jnp.zeros_like(acc_scr)

    def step(j, masked):
        kb = k_ref[0, pl.ds(pl.multiple_of(j * t, t), t), :]
        s = _dot_nt(q, kb)
        if masked:
            s = jnp.where(causal, s, NEG)
        m_old = m_scr[...]
        m_new = jnp.maximum(m_old, jnp.max(s, axis=1, keepdims=True))
        p = jnp.exp(s - jnp.concatenate([m_new] * (t // LANES), axis=1))
        alpha = jnp.exp(m_old - m_new)
        pv = _dot(_bf(p), jnp.concatenate([kb[:, :KV_LORA], ones], axis=1))
        acc_scr[...] = alpha * acc_scr[...] + pv[:, :KV_LORA]
        l_scr[...] = alpha * l_scr[...] + pv[:, KV_LORA:]
        m_scr[...] = m_new

    def body(j, carry):
        step(j, False)
        return carry

    lax.fori_loop(0, i, body, 0)
    step(i, True)
    o = acc_scr[...] / l_scr[...]
    for h in range(nh):
        o_ref[0, :, h * KV_LORA:(h + 1) * KV_LORA] = o[h * t:(h + 1) * t].astype(o_ref.dtype)


def mla_prompt(q_cat, k_cat, tq):
    b, t, _ = q_cat.shape
    rows = MLA_HEADS * tq
    return pl.pallas_call(
        functools.partial(_mla_prompt_kernel, t=tq),
        grid=(b, t // tq),
        in_specs=[pl.BlockSpec((1, tq, MLA_HEADS * QCAT), lambda i, j: (i, j, 0)),
                  pl.BlockSpec((1, t, QCAT), lambda i, j: (i, 0, 0))],
        out_specs=pl.BlockSpec((1, tq, MLA_HEADS * KV_LORA), lambda i, j: (i, j, 0)),
        out_shape=jax.ShapeDtypeStruct((b, t, MLA_HEADS * KV_LORA), BF16),
        scratch_shapes=[pltpu.VMEM((rows, LANES), F32), pltpu.VMEM((rows, LANES), F32), pltpu.VMEM((rows, KV_LORA), F32)],
        compiler_params=_cp(("parallel", "arbitrary")),
        name="mla_prompt",
    )(q_cat, k_cat)


def _mla_sample_kernel(pt_ref, q_ref, knew_ref, ckv_hbm, krt_hbm, o_ref, ckv_buf, kr_buf, s_scr, sem,
                       *, layer, n_pages, n_seq):
    b = pl.program_id(0)
    slot = b % 2

    def start_page(seq, sl, i):
        page = pt_ref[seq * n_pages + i]
        pltpu.make_async_copy(ckv_hbm.at[layer, page], ckv_buf.at[sl, i], sem.at[sl, 0]).start()
        pltpu.make_async_copy(krt_hbm.at[layer, page], kr_buf.at[sl, i], sem.at[sl, 1]).start()

    @pl.when(b == 0)
    def _():
        def body(i, carry):
            start_page(0, 0, i)
            return carry
        lax.fori_loop(0, n_pages, body, 0)

    pltpu.make_async_copy(ckv_hbm.at[layer, pl.ds(0, n_pages)], ckv_buf.at[slot], sem.at[slot, 0]).wait()
    pltpu.make_async_copy(krt_hbm.at[layer, pl.ds(0, n_pages)], kr_buf.at[slot], sem.at[slot, 1]).wait()

    q = q_ref[0]
    ql = q[:, :KV_LORA]
    qr = q[:, KV_LORA:KV_LORA + QK_ROPE_DIM]
    n_groups = n_pages // PAGE_GROUP

    def score_pages(prefetch):
        def score_body(gi, carry):
            for i in range(PAGE_GROUP):
                pi = gi * PAGE_GROUP + i
                if prefetch:
                    start_page(b + 1, 1 - slot, pi)
                s_scr[pi] = _dot_nt(ql, _bf(ckv_buf[slot, pi])) + _dot(qr, _bf(kr_buf[slot, pi]))
            return carry
        lax.fori_loop(0, n_groups, score_body, 0)

    @pl.when(b + 1 < n_seq)
    def _():
        score_pages(True)

    @pl.when(b + 1 == n_seq)
    def _():
        score_pages(False)

    k_new = knew_ref[0].astype(F32)
    s_new = jnp.sum(q.astype(F32) * k_new, axis=1, keepdims=True)
    s = s_scr[...]
    m = jnp.maximum(jnp.max(jnp.max(s, axis=0), axis=1, keepdims=True), s_new)
    p = jnp.exp(s - m)
    p_new = jnp.exp(s_new - m)
    l = jnp.sum(jnp.sum(p, axis=0), axis=1, keepdims=True) + p_new
    s_scr[...] = p

    def pv_body(gi, acc):
        for i in range(PAGE_GROUP):
            pi = gi * PAGE_GROUP + i
            acc = acc + _dot(_bf(s_scr[pi]), _bf(ckv_buf[slot, pi]))
        return acc
    acc = lax.fori_loop(0, n_groups, pv_body, jnp.zeros((MLA_HEADS, KV_LORA), F32))
    o_ref[0] = ((acc + p_new * k_new[:, :KV_LORA]) / l).astype(o_ref.dtype)


def mla_sample(page_table_flat, q, k_new, cache_ckv, cache_krope_t, layer, n_pages):
    n_seq = q.shape[0]
    grid_spec = pltpu.PrefetchScalarGridSpec(
        num_scalar_prefetch=1,
        grid=(n_seq,),
        in_specs=[pl.BlockSpec((1, MLA_HEADS, QCAT), lambda bi, pt: (bi, 0, 0)),
                  pl.BlockSpec((1, 1, QCAT), lambda bi, pt: (bi, 0, 0)),
                  pl.BlockSpec(memory_space=pl.ANY),
                  pl.BlockSpec(memory_space=pl.ANY)],
        out_specs=pl.BlockSpec((1, MLA_HEADS, KV_LORA), lambda bi, pt: (bi, 0, 0)),
        scratch_shapes=[pltpu.VMEM((2, n_pages, PAGE_SIZE, KV_LORA), F32),
                        pltpu.VMEM((2, n_pages, QK_ROPE_DIM, PAGE_SIZE), F32),
                        pltpu.VMEM((n_pages, MLA_HEADS, PAGE_SIZE), F32),
                        pltpu.SemaphoreType.DMA((2, 2))],
    )
    return pl.pallas_call(
        functools.partial(_mla_sample_kernel, layer=layer, n_pages=n_pages, n_seq=n_seq),
        grid_spec=grid_spec,
        out_shape=jax.ShapeDtypeStruct((n_seq, MLA_HEADS, KV_LORA), BF16),
        compiler_params=_cp(("arbitrary",)),
        name="mla_sample",
    )(page_table_flat, q, k_new, cache_ckv, cache_krope_t)


def _cross_kernel(q_ref, mk_ref, mv_ref, o_ref):
    scale = MEM_HEAD_DIM ** -0.5
    for h in range(MEM_HEADS):
        hs = slice(h * MEM_HEAD_DIM, (h + 1) * MEM_HEAD_DIM)
        q = q_ref[0, :, hs]
        kh = _bf(mk_ref[0, :, hs])
        vh = _bf(mv_ref[0, :, hs])
        s = _dot_nt(q, kh) * scale
        m = jnp.max(s, axis=1, keepdims=True)
        p = jnp.exp(s - m)
        l = jnp.sum(p, axis=1, keepdims=True)
        o = _dot(_bf(p), vh) / l
        o_ref[0, :, hs] = o.astype(o_ref.dtype)


def cross_attend(q, mem_k, mem_v, tq):
    b, t, d = q.shape
    return pl.pallas_call(
        _cross_kernel,
        grid=(b, t // tq),
        in_specs=[pl.BlockSpec((1, tq, d), lambda i, j: (i, j, 0)),
                  pl.BlockSpec((1, N_MEM, d), lambda i, j: (i, 0, 0)),
                  pl.BlockSpec((1, N_MEM, d), lambda i, j: (i, 0, 0))],
        out_specs=pl.BlockSpec((1, tq, d), lambda i, j: (i, j, 0)),
        out_shape=jax.ShapeDtypeStruct((b, t, d), BF16),
        compiler_params=_cp(("parallel", "arbitrary")),
        name="cross_attend",
    )(q, mem_k, mem_v)


def _cross_sample_kernel(q_ref, mk_ref, mv_ref, o_ref):
    scale = MEM_HEAD_DIM ** -0.5
    q = q_ref[0].astype(F32)
    q4 = jnp.concatenate([q[:, h * MEM_HEAD_DIM:(h + 1) * MEM_HEAD_DIM] for h in range(MEM_HEADS)], axis=0)
    s = jnp.sum(mk_ref[0, 0] * q4[None], axis=2, keepdims=True) * scale
    m = jnp.max(s, axis=0, keepdims=True)
    p = jnp.exp(s - m)
    l = jnp.sum(p, axis=0)
    o = jnp.sum(p * mv_ref[0, 0], axis=0) / l
    for h in range(MEM_HEADS):
        o_ref[0, :, h * MEM_HEAD_DIM:(h + 1) * MEM_HEAD_DIM] = o[h:h + 1].astype(o_ref.dtype)


def cross_sample(q, mem_k, mem_v, layer):
    b, _, d = q.shape
    blk = pl.BlockSpec((1, 1, N_MEM, MEM_HEADS, MEM_HEAD_DIM), lambda i: (layer, i, 0, 0, 0))
    return pl.pallas_call(
        _cross_sample_kernel,
        grid=(b,),
        in_specs=[pl.BlockSpec((1, 1, d), lambda i: (i, 0, 0)), blk, blk],
        out_specs=pl.BlockSpec((1, 1, d), lambda i: (i, 0, 0)),
        out_shape=jax.ShapeDtypeStruct((b, 1, d), BF16),
        compiler_params=_cp(("parallel",)),
        name="cross_sample",
    )(q, mem_k, mem_v)


def _rot_cols(w):
    half = QK_ROPE_DIM // 2
    return jnp.concatenate([-w[..., half:], w[..., :half]], axis=-1)


def _pad_lanes(w):
    return jnp.pad(w, [(0, 0)] * (w.ndim - 1) + [(0, LANES - w.shape[-1])])


def _layer_weights(l, w_in, shift_mu, rwkv_w0, rwkv_w_up, rwkv_a0, rwkv_a_up, rwkv_g_up, rwkv_k_k, rwkv_k_a,
                   rwkv_r_k, rwkv_gn_w, rwkv_gn_b, mla_q_norm, mla_w_uq, mla_kv_norm, mla_w_uk, mla_w_uv):
    w3 = RWKV_WIDTH
    o_kr = RWKV_PROJ + Q_LORA + KV_LORA
    w_kr = w_in[l][:, o_kr:]
    w_in_ext = _bf(jnp.concatenate([w_in[l][:, :o_kr], _pad_lanes(w_kr), _pad_lanes(_rot_cols(w_kr))], axis=1))
    w_lora = jnp.zeros((LORA_SLAB, 3 * w3), F32)
    w_lora = w_lora.at[:DECAY_LORA, :w3].set(rwkv_w_up[l])
    w_lora = w_lora.at[DECAY_LORA:DECAY_LORA + ICLR_LORA, w3:2 * w3].set(rwkv_a_up[l])
    w_lora = w_lora.at[DECAY_LORA + ICLR_LORA:, 2 * w3:].set(rwkv_g_up[l])
    uq = mla_w_uq[l].reshape(Q_LORA, MLA_HEADS, QK_NOPE_DIM + QK_ROPE_DIM)
    uq_nope = uq[:, :, :QK_NOPE_DIM].reshape(Q_LORA, MLA_HEADS * QK_NOPE_DIM)
    uq_rope = uq[:, :, QK_NOPE_DIM:]
    w_q = _bf(jnp.concatenate([uq_nope,
                               _pad_lanes(uq_rope).reshape(Q_LORA, MLA_HEADS * LANES),
                               _pad_lanes(_rot_cols(uq_rope)).reshape(Q_LORA, MLA_HEADS * LANES)], axis=1))
    eye_h = jnp.eye(MLA_HEADS, dtype=F32)
    wuk_bd = _bf(jnp.einsum('chd,hg->hdgc', mla_w_uk[l], eye_h).reshape(MLA_HEADS * QK_NOPE_DIM, MLA_HEADS * KV_LORA))
    wuv_bd = _bf(jnp.einsum('chd,hg->hcgd', mla_w_uv[l], eye_h).reshape(MLA_HEADS * KV_LORA, MLA_HEADS * V_HEAD_DIM))
    row = lambda a: a.reshape(1, -1)
    return dict(
        w_in_ext=w_in_ext, mu=row(shift_mu[l]), w0=row(rwkv_w0[l]), a0=row(rwkv_a0[l]), w_lora=_bf(w_lora),
        k_k=row(rwkv_k_k[l]), k_a=row(rwkv_k_a[l]), rk=row(rwkv_r_k[l]), gnw=row(rwkv_gn_w[l]), gnb=row(rwkv_gn_b[l]),
        q_norm=row(mla_q_norm[l]), kv_norm=row(mla_kv_norm[l]), w_q=w_q, wuk_bd=wuk_bd, wuv_bd=wuv_bd)


def _rope_tables(pos):
    half = QK_ROPE_DIM // 2
    inv_freq = ROPE_THETA ** (-jnp.arange(half, dtype=F32) / half)
    ang = pos.astype(F32)[:, None] * inv_freq[None, :]
    cos, sin = jnp.cos(ang), jnp.sin(ang)
    return (_pad_lanes(jnp.concatenate([cos, cos], axis=1)), _pad_lanes(jnp.concatenate([sin, sin], axis=1)))


def kernel(x_prompt, x_sample, cache_ckv, cache_krope, state_wkv, state_shift, cache_mem_k, cache_mem_v, page_table, mem_prompt, norm_ffn1, ffn1_w_gate, ffn1_w_up, ffn1_w_down, norm_mix, w_in, shift_mu, rwkv_w0, rwkv_w_up, rwkv_a0, rwkv_a_up, rwkv_g_up, rwkv_k_k, rwkv_k_a, rwkv_r_k, rwkv_gn_w, rwkv_gn_b, mla_q_norm, mla_w_uq, mla_kv_norm, mla_w_uk, mla_w_uv, w_out, norm_cross, norm_mem, mem_w_q, mem_w_k, mem_w_v, mem_w_o, norm_ffn2, ffn2_w_gate, ffn2_w_up, ffn2_w_down, norm_final):
    bp, tp, d = x_prompt.shape
    bs, ts, _ = x_sample.shape
    assert ts == 1
    n_pages = page_table.shape[1]
    assert n_pages % PAGE_GROUP == 0
    past_len = n_pages * PAGE_SIZE
    hh, nn = RWKV_HEADS, RWKV_HEAD_DIM
    row = lambda a: a.reshape(1, -1)

    head_of = jnp.arange(RWKV_WIDTH) // nn
    ones_bd = (head_of[:, None] == head_of[None, :]).astype(BF16)
    cos_p, sin_p = _rope_tables(jnp.arange(tp))
    cos_s, sin_s = _rope_tables(jnp.full((bs,), past_len, jnp.int32))
    pt_flat = page_table.reshape(-1)
    krope_t = jnp.swapaxes(cache_krope, 2, 3)

    lw = [_layer_weights(l, w_in, shift_mu, rwkv_w0, rwkv_w_up, rwkv_a0, rwkv_a_up, rwkv_g_up, rwkv_k_k, rwkv_k_a,
                         rwkv_r_k, rwkv_gn_w, rwkv_gn_b, mla_q_norm, mla_w_uq, mla_kv_norm, mla_w_uk, mla_w_uv)
          for l in range(DEPTH)]

    tm_p, tm_s = 512, bs
    tf = D_FF // 2

    def run_ffn(x2, l, which, tm, final):
        if which == 1:
            g, wg, wu, wd = norm_ffn1[l], ffn1_w_gate[l], ffn1_w_up[l], ffn1_w_down[l]
        else:
            g, wg, wu, wd = norm_ffn2[l], ffn2_w_gate[l], ffn2_w_up[l], ffn2_w_down[l]
        return ffn(x2, row(g), _bf(wg), _bf(wu), _bf(wd), row(norm_final), final, tm, tf)

    mem2 = mem_prompt.reshape(bp * N_MEM, d)
    mem_k_p, mem_v_p = [], []
    for l in range(DEPTH):
        kv = norm_matmul(mem2, row(norm_mem[l]), _bf(jnp.concatenate([mem_w_k[l], mem_w_v[l]], axis=1)), F32, 512)
        mem_k_p.append(kv[:, :d].reshape(bp, N_MEM, d))
        mem_v_p.append(kv[:, d:].reshape(bp, N_MEM, d))

    def trunk(x, sample):
        b, t, _ = x.shape
        m = b * t
        tm = tm_s if sample else tm_p
        x2 = x.reshape(m, d)
        ckvs, krs, wkvs, shifts = [], [], [], []
        for l in range(DEPTH):
            p = lw[l]
            x2 = run_ffn(x2, l, 1, tm, False)
            proj = norm_matmul(x2, row(norm_mix[l]), p['w_in_ext'], F32, tm)
            if sample:
                outs = prep(proj.reshape(1, m, IN_PROJ_EXT), state_shift[l].reshape(1, m, RWKV_PROJ), True,
                            p['mu'], p['w0'], p['a0'], p['w_lora'], p['k_k'], p['k_a'], ones_bd, p['q_norm'],
                            p['kv_norm'], p['w_q'], p['wuk_bd'], cos_s, sin_s, m)
                shift_last = proj[:, :RWKV_PROJ]
            else:
                outs = prep(proj.reshape(b, t, IN_PROJ_EXT), jnp.zeros((b, 1, RWKV_PROJ), F32), False,
                            p['mu'], p['w0'], p['a0'], p['w_lora'], p['k_k'], p['k_a'], ones_bd, p['q_norm'],
                            p['kv_norm'], p['w_q'], p['wuk_bd'], cos_p, sin_p, 256)
                shift_last = outs[11].reshape(b, RWKV_PROJ)
            r, k, v, kk, bv, lwd, g, ckv, kr, q_cat, k_cat = outs[:11]
            if sample:
                to_heads = lambda a: a.reshape(m, hh, nn)
                o_rwkv, wkv = rwkv_step(to_heads(r), to_heads(k), to_heads(v), to_heads(kk), to_heads(bv),
                                        to_heads(lwd), to_heads(g), p['rk'].reshape(hh, nn), p['gnw'].reshape(hh, nn),
                                        p['gnb'].reshape(hh, nn), state_wkv[l], 8)
                o_rwkv = _bf(o_rwkv.reshape(m, RWKV_WIDTH))
                o_lat = mla_sample(pt_flat, q_cat.reshape(m, MLA_HEADS, QCAT), k_cat.reshape(m, 1, QCAT),
                                   cache_ckv, krope_t, l, n_pages).reshape(m, MLA_HEADS * KV_LORA)
            else:
                o_rwkv, s_bd = rwkv_chunked(r, k, v, kk, bv, lwd, g, p['rk'], p['gnw'], p['gnb'], CHUNK, CHUNK_SEQS)
                o_rwkv = o_rwkv.reshape(m, RWKV_WIDTH)
                wkv = jnp.stack([s_bd[:, :, :nn, :nn], s_bd[:, :, nn:, nn:]], axis=2).reshape(b, hh, nn, nn)
                o_lat = mla_prompt(q_cat, k_cat, 256).reshape(m, MLA_HEADS * KV_LORA)
            x2 = mix_out(o_rwkv, o_lat, p['wuv_bd'], _bf(w_out[l]), x2, tm)
            if sample:
                q = norm_matmul(x2, row(norm_cross[l]), _bf(mem_w_q[l]), F32, tm)
                att = cross_sample(q.reshape(b, t, d), cache_mem_k, cache_mem_v, l)
            else:
                q = norm_matmul(x2, row(norm_cross[l]), _bf(mem_w_q[l]), BF16, tm)
                att = cross_attend(q.reshape(b, t, d), mem_k_p[l], mem_v_p[l], 512)
            x2 = matmul_res(att.reshape(m, d), _bf(mem_w_o[l]), x2, tm)
            x2 = run_ffn(x2, l, 2, tm, l == DEPTH - 1)
            ckvs.append(ckv.reshape(b, t, KV_LORA))
            krs.append(kr.reshape(b, t, QK_ROPE_DIM))
            wkvs.append(wkv)
            shifts.append(shift_last)
        return x2.reshape(b, t, d), jnp.stack(ckvs), jnp.stack(krs), jnp.stack(wkvs), jnp.stack(shifts)

    y_p, ckv_p, kr_p, wkv_p, shift_p = trunk(x_prompt, False)
    y_s, ckv_s, kr_s, wkv_s, shift_s = trunk(x_sample, True)
    mem_shape = (DEPTH, bp, N_MEM, MEM_HEADS, MEM_HEAD_DIM)
    return (y_p, y_s, ckv_p, kr_p, wkv_p, shift_p,
            jnp.stack(mem_k_p).reshape(mem_shape), jnp.stack(mem_v_p).reshape(mem_shape),
            ckv_s, kr_s, wkv_s, shift_s)
```

```python
import functools

import jax
import jax.numpy as jnp
from jax import lax
from jax.experimental import pallas as pl
from jax.experimental.pallas import tpu as pltpu

F32 = jnp.float32
BF16 = jnp.bfloat16

D_MODEL = 1024
DEPTH = 2
PAGE_SIZE = 128
RWKV_HEADS = 8
RWKV_HEAD_DIM = 64
RWKV_WIDTH = 512
DECAY_LORA = 64
ICLR_LORA = 64
GATE_LORA = 128
GN_EPS = 64e-5
MLA_HEADS = 8
QK_NOPE_DIM = 64
QK_ROPE_DIM = 32
V_HEAD_DIM = 64
Q_LORA = 256
KV_LORA = 128
ROPE_THETA = 10000.0
MLA_SCALE = (QK_NOPE_DIM + QK_ROPE_DIM) ** -0.5
RWKV_PROJ = 3 * RWKV_WIDTH + DECAY_LORA + ICLR_LORA + GATE_LORA
N_MEM = 256
MEM_HEADS = 4
MEM_HEAD_DIM = D_MODEL // MEM_HEADS
D_FF = 2816
RMS_EPS = 1e-6

LANES = 128
IN_PROJ_EXT = RWKV_PROJ + Q_LORA + KV_LORA + 2 * LANES
LORA_SLAB = DECAY_LORA + ICLR_LORA + GATE_LORA
QCAT = 2 * LANES
CHUNK = 64
CHUNK_SEQS = 4
FFN_CHUNK = 256
PAGE_GROUP = 32
NEG = -1e30
VMEM_LIMIT = 56 * 1024 * 1024


def _cp(sem):
    return pltpu.CompilerParams(dimension_semantics=sem, vmem_limit_bytes=VMEM_LIMIT)


def _dot(a, b):
    return jnp.dot(a, b, preferred_element_type=F32)


def _dot_nt(a, b):
    return lax.dot_general(a, b, (((1,), (1,)), ((), ())), preferred_element_type=F32)


def _bf(x):
    return x.astype(BF16)


def _split3(x):
    hi = _bf(x)
    r1 = x - hi.astype(F32)
    mid = _bf(r1)
    return hi, mid, _bf(r1 - mid.astype(F32))


def _dot_split2(x, w):
    hi = _bf(x)
    return _dot(hi, w) + _dot(_bf(x - hi.astype(F32)), w)


def _rms(x, g):
    return x * lax.rsqrt(jnp.mean(x * x, axis=-1, keepdims=True) + RMS_EPS) * g


def _sigmoid(x):
    return 1.0 / (1.0 + jnp.exp(-x))


def _norm_matmul_kernel(x_ref, g_ref, w_ref, o_ref):
    h = _bf(_rms(x_ref[...], g_ref[...]))
    o_ref[...] = _dot(h, w_ref[...]).astype(o_ref.dtype)


def norm_matmul(x, g, w, out_dtype, tm):
    m, d = x.shape
    n = w.shape[1]
    return pl.pallas_call(
        _norm_matmul_kernel,
        grid=(m // tm,),
        in_specs=[pl.BlockSpec((tm, d), lambda i: (i, 0)),
                  pl.BlockSpec((1, d), lambda i: (0, 0)),
                  pl.BlockSpec((d, n), lambda i: (0, 0))],
        out_specs=pl.BlockSpec((tm, n), lambda i: (i, 0)),
        out_shape=jax.ShapeDtypeStruct((m, n), out_dtype),
        compiler_params=_cp(("parallel",)),
        name="norm_matmul",
    )(x, g, w)


def _ffn_kernel(x_ref, g_ref, wg_ref, wu_ref, wd_ref, gf_ref, o_ref, act_scr, *, final_norm):
    x = x_ref[...]
    h = _bf(_rms(x, g_ref[...]))
    dff = wg_ref.shape[1]
    for c0 in range(0, dff, FFN_CHUNK):
        cs = slice(c0, min(c0 + FFN_CHUNK, dff))
        gate = _dot(h, wg_ref[:, cs])
        up = _dot(h, wu_ref[:, cs])
        act_scr[:, cs] = _bf(gate * _sigmoid(gate) * up)
    y = x + 0.5 * _dot(act_scr[...], wd_ref[...])
    if final_norm:
        y = _rms(y, gf_ref[...])
    o_ref[...] = y


def _resident(shape):
    return pl.BlockSpec(shape, lambda *_: (0,) * len(shape), pipeline_mode=pl.Buffered(1))


def ffn(x, g, wg, wu, wd, gf, final_norm, tm):
    m, d = x.shape
    dff = wg.shape[1]
    return pl.pallas_call(
        functools.partial(_ffn_kernel, final_norm=final_norm),
        grid=(m // tm,),
        in_specs=[pl.BlockSpec((tm, d), lambda i: (i, 0)), _resident((1, d)), _resident((d, dff)),
                  _resident((d, dff)), _resident((dff, d)), _resident((1, d))],
        out_specs=pl.BlockSpec((tm, d), lambda i: (i, 0)),
        out_shape=jax.ShapeDtypeStruct((m, d), F32),
        scratch_shapes=[pltpu.VMEM((tm, dff), BF16)],
        compiler_params=_cp(("parallel",)),
        name="ffn",
    )(x, g, wg, wu, wd, gf)


def _matmul_res_kernel(a_ref, w_ref, x_ref, o_ref):
    o_ref[...] = x_ref[...] + _dot(a_ref[...], w_ref[...])


def matmul_res(a, w, x, tm):
    m, k = a.shape
    n = w.shape[1]
    return pl.pallas_call(
        _matmul_res_kernel,
        grid=(m // tm,),
        in_specs=[pl.BlockSpec((tm, k), lambda i: (i, 0)),
                  pl.BlockSpec((k, n), lambda i: (0, 0)),
                  pl.BlockSpec((tm, n), lambda i: (i, 0))],
        out_specs=pl.BlockSpec((tm, n), lambda i: (i, 0)),
        out_shape=jax.ShapeDtypeStruct((m, n), F32),
        compiler_params=_cp(("parallel",)),
        name="matmul_res",
    )(a, w, x)


def _mix_out_kernel(orw_ref, olat_ref, wuv_ref, wout_ref, x_ref, o_ref):
    o_mla = _bf(_dot(olat_ref[...], wuv_ref[...]))
    y = _dot(orw_ref[...], wout_ref[:RWKV_WIDTH, :]) + _dot(o_mla, wout_ref[RWKV_WIDTH:, :])
    o_ref[...] = x_ref[...] + y


def mix_out(o_rwkv, o_lat, wuv_bd, w_out, x, tm):
    m, d = x.shape
    return pl.pallas_call(
        _mix_out_kernel,
        grid=(m // tm,),
        in_specs=[pl.BlockSpec((tm, RWKV_WIDTH), lambda i: (i, 0)),
                  pl.BlockSpec((tm, MLA_HEADS * KV_LORA), lambda i: (i, 0)),
                  pl.BlockSpec(wuv_bd.shape, lambda i: (0, 0)),
                  pl.BlockSpec(w_out.shape, lambda i: (0, 0)),
                  pl.BlockSpec((tm, d), lambda i: (i, 0))],
        out_specs=pl.BlockSpec((tm, d), lambda i: (i, 0)),
        out_shape=jax.ShapeDtypeStruct((m, d), F32),
        compiler_params=_cp(("parallel",)),
        name="mix_out",
    )(o_rwkv, o_lat, wuv_bd, w_out, x)


def _post_mix_kernel(orw_ref, olat_ref, x_ref, wuv_ref, wout_ref, gc_ref, wq_ref, mk_ref, mv_ref, wo_ref, o_ref):
    o_mla = _bf(_dot(olat_ref[0], wuv_ref[...]))
    x1 = x_ref[0] + _dot(orw_ref[0], wout_ref[:RWKV_WIDTH, :]) + _dot(o_mla, wout_ref[RWKV_WIDTH:, :])
    q = _bf(_dot(_bf(_rms(x1, gc_ref[...])), wq_ref[...]))
    scale = MEM_HEAD_DIM ** -0.5
    att = []
    for h in range(MEM_HEADS):
        hs = slice(h * MEM_HEAD_DIM, (h + 1) * MEM_HEAD_DIM)
        s = _dot_nt(q[:, hs], _bf(mk_ref[0, :, hs])) * scale
        p = jnp.exp(s - jnp.max(s, axis=1, keepdims=True))
        l = jnp.sum(p, axis=1, keepdims=True)
        att.append(_bf(_dot(_bf(p), _bf(mv_ref[0, :, hs])) / l))
    o_ref[0] = x1 + _dot(jnp.concatenate(att, axis=1), wo_ref[...])


def post_mix(o_rwkv, o_lat, x, wuv_bd, w_out, g_cross, w_q, mem_k, mem_v, w_o, tq):
    b, t, d = x.shape
    seq = lambda w: pl.BlockSpec((1, tq, w), lambda i, j: (i, j, 0))
    mem = pl.BlockSpec((1, N_MEM, d), lambda i, j: (i, 0, 0))
    return pl.pallas_call(
        _post_mix_kernel,
        grid=(b, t // tq),
        in_specs=[seq(RWKV_WIDTH), seq(MLA_HEADS * KV_LORA), seq(d), _resident(wuv_bd.shape), _resident(w_out.shape),
                  _resident(g_cross.shape), _resident(w_q.shape), mem, mem, _resident(w_o.shape)],
        out_specs=seq(d),
        out_shape=jax.ShapeDtypeStruct((b, t, d), F32),
        compiler_params=_cp(("parallel", "arbitrary")),
        name="post_mix",
    )(o_rwkv, o_lat, x, wuv_bd, w_out, g_cross, w_q, mem_k, mem_v, w_o)


def _prep_kernel(x_ref, gmix_ref, win_ref, prev_ref, mu_ref, w0_ref, a0_ref, wl_ref, kk_ref, ka_ref, ones_ref,
                 qn_ref, kvn_ref, wq_ref, wuk_ref, cos_ref, sin_ref,
                 r_o, k_o, v_o, kkn_o, b_o, lw_o, g_o, ckv_o, kr_o, qcat_o, kcat_o, raw_o,
                 proj_ref, carry, *, tm, explicit_prev):
    proj_ref[...] = _dot(_bf(_rms(x_ref[0], gmix_ref[...])), win_ref[...])
    p = proj_ref[:, :RWKV_PROJ]
    if explicit_prev:
        prev = prev_ref[0]
        raw_o[0] = p
    else:
        @pl.when(pl.program_id(1) == 0)
        def _():
            carry[...] = prev_ref[0]

        row = lax.broadcasted_iota(jnp.int32, (tm, 1), 0)
        prev = jnp.where(row == 0, carry[...], pltpu.roll(p, 1, axis=0))
        carry[...] = p[tm - 1:tm, :]
        raw_o[0] = p[tm - 1:tm, :]
    ps = p + (prev - p) * mu_ref[...]

    w3 = RWKV_WIDTH
    r = ps[:, :w3]
    k = ps[:, w3:2 * w3]
    v = ps[:, 2 * w3:3 * w3]
    slab = ps[:, 3 * w3:]
    lane = lax.broadcasted_iota(jnp.int32, slab.shape, 1)
    act = jnp.where(lane < DECAY_LORA, jnp.tanh(slab),
                    jnp.where(lane < DECAY_LORA + ICLR_LORA, slab, _sigmoid(slab)))
    lo = _dot_split2(act, wl_ref[...])
    z = -(w0_ref[...] + lo[:, :w3])
    softplus = jnp.maximum(z, 0.0) + jnp.log(1.0 + jnp.exp(-jnp.abs(z)))
    lw = -jnp.exp(-softplus - 0.5)
    a = _sigmoid(a0_ref[...] + lo[:, w3:2 * w3])
    g = lo[:, 2 * w3:]
    kkr = k * kk_ref[...]
    ssq = _dot_split2(kkr * kkr, ones_ref[...])
    kkn = kkr / jnp.maximum(jnp.sqrt(ssq), 1e-12)
    r_o[0] = r
    k_o[0] = k * (1.0 + (a - 1.0) * ka_ref[...])
    v_o[0] = v
    kkn_o[0] = kkn
    b_o[0] = kkn * a
    lw_o[0] = lw
    g_o[0] = g

    o1 = RWKV_PROJ
    cq = _bf(_rms(proj_ref[:, o1:o1 + Q_LORA], qn_ref[...]))
    o2 = o1 + Q_LORA
    ckv = _rms(proj_ref[:, o2:o2 + KV_LORA], kvn_ref[...])
    o3 = o2 + KV_LORA
    cos = cos_ref[...]
    sin = sin_ref[...]
    kr = proj_ref[:, o3:o3 + LANES] * cos + proj_ref[:, o3 + LANES:o3 + 2 * LANES] * sin
    ckv_o[0] = ckv
    kr_o[0] = kr[:, :QK_ROPE_DIM]
    kcat_o[0] = _bf(jnp.concatenate([ckv, kr], axis=1))

    qq = _dot(cq, wq_ref[...])
    nn = MLA_HEADS * QK_NOPE_DIM
    nr = MLA_HEADS * LANES
    q_lat = _dot(_bf(qq[:, :nn]), wuk_ref[...]) * MLA_SCALE
    cos8 = jnp.concatenate([cos] * MLA_HEADS, axis=1)
    sin8 = jnp.concatenate([sin] * MLA_HEADS, axis=1)
    q_rp = (qq[:, nn:nn + nr] * cos8 + qq[:, nn + nr:] * sin8) * MLA_SCALE
    pieces = []
    for h in range(MLA_HEADS):
        pieces.append(q_lat[:, h * LANES:(h + 1) * LANES])
        pieces.append(q_rp[:, h * LANES:(h + 1) * LANES])
    qcat_o[0] = _bf(jnp.concatenate(pieces, axis=1))


def prep(x, g_mix, w_in_ext, prev, explicit_prev, mu, w0, a0, w_lora, k_k, k_a, ones_bd, q_norm, kv_norm, w_q, wuk_bd,
         cos_t, sin_t, tm):
    b, t, d = x.shape
    nt = t // tm
    wide = lambda w: pl.BlockSpec((1, tm, w), lambda i, j: (i, j, 0))
    full = lambda arr: _resident(arr.shape)
    if explicit_prev:
        prev_spec = wide(RWKV_PROJ)
        raw_spec, raw_rows = wide(RWKV_PROJ), t
    else:
        prev_spec = pl.BlockSpec((1, 1, RWKV_PROJ), lambda i, j: (i, 0, 0))
        raw_spec, raw_rows = pl.BlockSpec((1, 1, RWKV_PROJ), lambda i, j: (i, 0, 0)), 1
    sd = lambda w, dt: jax.ShapeDtypeStruct((b, t, w), dt)
    w3 = RWKV_WIDTH
    out_shape = [sd(w3, F32)] * 7 + [sd(KV_LORA, F32), sd(QK_ROPE_DIM, F32),
                                      sd(MLA_HEADS * QCAT, BF16), sd(QCAT, BF16),
                                      jax.ShapeDtypeStruct((b, raw_rows, RWKV_PROJ), F32)]
    out_specs = [wide(w3)] * 7 + [wide(KV_LORA), wide(QK_ROPE_DIM), wide(MLA_HEADS * QCAT), wide(QCAT), raw_spec]
    return pl.pallas_call(
        functools.partial(_prep_kernel, tm=tm, explicit_prev=explicit_prev),
        grid=(b, nt),
        in_specs=[wide(d), full(g_mix), full(w_in_ext), prev_spec, full(mu), full(w0), full(a0), full(w_lora),
                  full(k_k), full(k_a), full(ones_bd), full(q_norm), full(kv_norm), full(w_q), full(wuk_bd),
                  pl.BlockSpec((tm, LANES), lambda i, j: (j, 0)),
                  pl.BlockSpec((tm, LANES), lambda i, j: (j, 0))],
        out_specs=out_specs,
        out_shape=out_shape,
        scratch_shapes=[pltpu.VMEM((tm, IN_PROJ_EXT), F32), pltpu.VMEM((1, RWKV_PROJ), F32)],
        compiler_params=_cp(("arbitrary", "arbitrary")),
        name="prep",
    )(x, g_mix, w_in_ext, prev, mu, w0, a0, w_lora, k_k, k_a, ones_bd, q_norm, kv_norm, w_q, wuk_bd, cos_t, sin_t)


def _rwkv_chunk_kernel(r_ref, k_ref, v_ref, kk_ref, b_ref, lw_ref, g_ref, rk_ref, gnw_ref, gnb_ref,
                       o_ref, s_out_ref, s_scr, *, c, n_chunks, nb):
    @pl.when(pl.program_id(1) == 0)
    def _():
        s_scr[...] = jnp.zeros_like(s_scr)

    n = RWKV_HEAD_DIM
    npair = RWKV_HEADS // 2
    row = lax.broadcasted_iota(jnp.int32, (c, c), 0)
    col = lax.broadcasted_iota(jnp.int32, (c, c), 1)
    incl = row >= col
    strict = row > col
    tri = _bf(jnp.where(incl, 1.0, 0.0))
    eye = jnp.where(row == col, 1.0, 0.0).astype(F32)
    lane = lax.broadcasted_iota(jnp.int32, (c, LANES), 1)
    is_e = lane < n
    r2 = lax.broadcasted_iota(jnp.int32, (LANES, LANES), 0)
    c2 = lax.broadcasted_iota(jnp.int32, (LANES, LANES), 1)
    same_head = (r2 < n) == (c2 < n)
    ones_bd = _bf(jnp.where(same_head, 1.0, 0.0))
    n_double = c.bit_length() - 2
    pairs = range(nb * npair)
    heads = [(j, x) for j in pairs for x in range(2)]
    sl = [slice((j % npair) * LANES, (j % npair + 1) * LANES) for j in pairs]
    bi = [j // npair for j in pairs]

    r = [r_ref[bi[j], :, sl[j]] for j in pairs]
    k = [k_ref[bi[j], :, sl[j]] for j in pairs]
    v = [v_ref[bi[j], :, sl[j]] for j in pairs]
    kk = [kk_ref[bi[j], :, sl[j]] for j in pairs]
    bv = [b_ref[bi[j], :, sl[j]] for j in pairs]
    lw = [lw_ref[bi[j], :, sl[j]] for j in pairs]
    s0 = [s_scr[bi[j], j % npair] for j in pairs]

    cum = [sum(_dot(tri, part) for part in _split3(lw[j])) for j in pairs]
    e_pos = [jnp.exp(cum[j]) for j in pairs]
    e_neg = [jnp.exp(-cum[j]) for j in pairs]
    a_t = [-kk[j] * jnp.exp(cum[j] - lw[j]) for j in pairs]
    r_t = [r[j] * e_pos[j] for j in pairs]
    b_t = [bv[j] * e_neg[j] for j in pairs]
    k_t = [k[j] * e_neg[j] for j in pairs]
    g_c = [e_pos[j][c - 1:c, :] for j in pairs]

    v_b = [_bf(v[j]) for j in pairs]
    s0_b = [_bf(s0[j]) for j in pairs]
    zero = jnp.zeros((c, LANES), F32)
    xb, xk = [], []
    for j in pairs:
        lhs = _bf(jnp.concatenate([jnp.where(is_e, a_t[j], zero), jnp.where(is_e, r_t[j], zero),
                                   jnp.where(is_e, zero, a_t[j]), jnp.where(is_e, zero, r_t[j])], axis=0))
        xb.append(_dot_nt(lhs, _bf(b_t[j])))
        xk.append(_dot_nt(lhs, _bf(k_t[j])))
    rs0 = [_dot_nt(_bf(r_t[j]), s0_b[j]) for j in pairs]

    nmat, arb, aakv, o2 = {}, {}, {}, {}
    for (j, x) in heads:
        base = 2 * c * x
        nmat[j, x] = jnp.where(strict, xb[j][base:base + c], 0.0)
        arb[j, x] = _bf(jnp.where(incl, xb[j][base + c:base + 2 * c], 0.0))
        aakv[j, x] = _dot(_bf(jnp.where(strict, xk[j][base:base + c], 0.0)), v_b[j])
        o2[j, x] = _dot(_bf(jnp.where(incl, xk[j][base + c:base + 2 * c], 0.0)), v_b[j])

    tinv = {h: eye + nmat[h] for h in heads}
    pw = dict(nmat)
    for _ in range(n_double):
        pw = {h: _dot(_bf(pw[h]), _bf(pw[h])) for h in heads}
        tinv = {h: tinv[h] + _dot(_bf(pw[h]), _bf(tinv[h])) for h in heads}

    w12 = {h: _dot(_bf(tinv[h]), _bf(jnp.concatenate([a_t[h[0]], aakv[h]], axis=1))) for h in heads}
    u = []
    for j in pairs:
        w1 = jnp.where(is_e, w12[j, 0][:, :LANES], w12[j, 1][:, :LANES])
        w2 = jnp.where(is_e, w12[j, 0][:, LANES:], w12[j, 1][:, LANES:])
        u.append(_dot_nt(_bf(w1), s0_b[j]) + w2)
    u_b = [_bf(u[j]) for j in pairs]
    o = [rs0[j] + jnp.where(is_e, _dot(arb[j, 0], u_b[j]) + o2[j, 0], _dot(arb[j, 1], u_b[j]) + o2[j, 1])
         for j in pairs]
    for j in pairs:
        uv = jnp.concatenate([u[j], v[j]], axis=0)
        bk = jnp.concatenate([b_t[j] * g_c[j], k_t[j] * g_c[j]], axis=0)
        s_scr[bi[j], j % npair] = s0[j] * g_c[j] + jnp.where(same_head, _dot(_bf(uv.T), _bf(bk)), 0.0)

    inv_n = 1.0 / n
    o_all = jnp.concatenate(o, axis=0)
    mean = _dot(_bf(o_all), ones_bd) * inv_n
    d = o_all - mean
    var = _dot(_bf(d * d), ones_bd) * inv_n
    bsum = _dot(_bf(jnp.concatenate([r[j] * k[j] * rk_ref[:, sl[j]] for j in pairs], axis=0)), ones_bd)
    on = d * lax.rsqrt(var + GN_EPS)
    for j in pairs:
        rows = slice(j * c, (j + 1) * c)
        out = (on[rows] * gnw_ref[:, sl[j]] + gnb_ref[:, sl[j]] + bsum[rows] * v[j]) * g_ref[bi[j], :, sl[j]]
        o_ref[bi[j], :, sl[j]] = out.astype(o_ref.dtype)

    @pl.when(pl.program_id(1) == n_chunks - 1)
    def _():
        s_out_ref[...] = s_scr[...]


def rwkv_chunked(r, k, v, kk, bv, lw, g, rk, gnw, gnb, c, nb):
    b, t, w = r.shape
    n_chunks = t // c
    seq = pl.BlockSpec((nb, c, w), lambda i, j: (i, j, 0))
    par = pl.BlockSpec((1, w), lambda i, j: (0, 0))
    npair = RWKV_HEADS // 2
    return pl.pallas_call(
        functools.partial(_rwkv_chunk_kernel, c=c, n_chunks=n_chunks, nb=nb),
        grid=(b // nb, n_chunks),
        in_specs=[seq] * 7 + [par] * 3,
        out_specs=[seq, pl.BlockSpec((nb, npair, LANES, LANES), lambda i, j: (i, 0, 0, 0))],
        out_shape=[jax.ShapeDtypeStruct((b, t, w), BF16),
                   jax.ShapeDtypeStruct((b, npair, LANES, LANES), F32)],
        scratch_shapes=[pltpu.VMEM((nb, npair, LANES, LANES), F32)],
        compiler_params=_cp(("arbitrary", "arbitrary")),
        name="rwkv_chunked",
    )(r, k, v, kk, bv, lw, g, rk, gnw, gnb)


def _rwkv_step_kernel(r_ref, k_ref, v_ref, kk_ref, b_ref, lw_ref, g_ref, rk_ref, gnw_ref, gnb_ref, s_ref,
                      o_ref, s_out_ref, *, bb):
    n = RWKV_HEAD_DIM
    eye = lax.broadcasted_iota(jnp.int32, (n, n), 0) == lax.broadcasted_iota(jnp.int32, (n, n), 1)

    def body(i, carry):
        for h in range(RWKV_HEADS):
            hs = slice(h, h + 1)
            s = s_ref[i, h]
            r = r_ref[i, hs, :]
            k = k_ref[i, hs, :]
            v = v_ref[i, hs, :]
            kk = kk_ref[i, hs, :]
            bv = b_ref[i, hs, :]
            w = jnp.exp(lw_ref[i, hs, :])
            s_kk = jnp.sum(s * kk, axis=1, keepdims=True)
            v_col = jnp.sum(jnp.where(eye, v, 0.0), axis=1, keepdims=True)
            s_new = s * w - s_kk * bv + v_col * k
            o_col = jnp.sum(s_new * r, axis=1, keepdims=True)
            o = jnp.sum(jnp.where(eye, o_col, 0.0), axis=0, keepdims=True)
            mean = jnp.mean(o, axis=1, keepdims=True)
            d = o - mean
            var = jnp.mean(d * d, axis=1, keepdims=True)
            on = d * lax.rsqrt(var + GN_EPS) * gnw_ref[hs, :] + gnb_ref[hs, :]
            bonus = jnp.sum(r * k * rk_ref[hs, :], axis=1, keepdims=True) * v
            o_ref[i, hs, :] = ((on + bonus) * g_ref[i, hs, :]).astype(o_ref.dtype)
            s_out_ref[i, h] = s_new
        return carry

    lax.fori_loop(0, bb, body, 0)


def rwkv_step(r, k, v, kk, bv, lw, g, rk, gnw, gnb, s, bb):
    b = r.shape[0]
    h, n = RWKV_HEADS, RWKV_HEAD_DIM
    vec = pl.BlockSpec((bb, h, n), lambda i: (i, 0, 0))
    par = pl.BlockSpec((h, n), lambda i: (0, 0))
    st = pl.BlockSpec((bb, h, n, n), lambda i: (i, 0, 0, 0))
    return pl.pallas_call(
        functools.partial(_rwkv_step_kernel, bb=bb),
        grid=(b // bb,),
        in_specs=[vec] * 7 + [par] * 3 + [st],
        out_specs=[vec, st],
        out_shape=[jax.ShapeDtypeStruct((b, h, n), F32), jax.ShapeDtypeStruct((b, h, n, n), F32)],
        compiler_params=_cp(("parallel",)),
        name="rwkv_step",
    )(r, k, v, kk, bv, lw, g, rk, gnw, gnb, s)


def _mla_prompt_kernel(q_ref, k_ref, o_ref, m_scr, l_scr, acc_scr, *, t):
    i = pl.program_id(1)
    nh = MLA_HEADS
    q = jnp.concatenate([q_ref[0, :, h * QCAT:(h + 1) * QCAT] for h in range(nh)], axis=0)
    row = lax.broadcasted_iota(jnp.int32, (t, 1), 0)
    causal = lax.broadcasted_iota(jnp.int32, (1, t), 1) <= jnp.concatenate([row] * nh, axis=0)
    ones = jnp.ones((t, LANES), BF16)
    m_scr[...] = jnp.full_like(m_scr, NEG)
    l_scr[...] = jnp.zeros_like(l_scr)
    acc_scr[...] = jnp.zeros_like(acc_scr)

    def step(j, masked):
        kb = k_ref[0, pl.ds(pl.multiple_of(j * t, t), t), :]
        s = _dot_nt(q, kb)
        if masked:
            s = jnp.where(causal, s, NEG)
        m_old = m_scr[...]
        m_new = jnp.maximum(m_old, jnp.max(s, axis=1, keepdims=True))
        p = jnp.exp(s - jnp.concatenate([m_new] * (t // LANES), axis=1))
        alpha = jnp.exp(m_old - m_new)
        pv = _dot(_bf(p), jnp.concatenate([kb[:, :KV_LORA], ones], axis=1))
        acc_scr[...] = alpha * acc_scr[...] + pv[:, :KV_LORA]
        l_scr[...] = alpha * l_scr[...] + pv[:, KV_LORA:]
        m_scr[...] = m_new

    def body(j, carry):
        step(j, False)
        return carry

    lax.fori_loop(0, i, body, 0)
    step(i, True)
    o = acc_scr[...] / l_scr[...]
    for h in range(nh):
        o_ref[0, :, h * KV_LORA:(h + 1) * KV_LORA] = o[h * t:(h + 1) * t].astype(o_ref.dtype)


def mla_prompt(q_cat, k_cat, tq):
    b, t, _ = q_cat.shape
    rows = MLA_HEADS * tq
    return pl.pallas_call(
        functools.partial(_mla_prompt_kernel, t=tq),
        grid=(b, t // tq),
        in_specs=[pl.BlockSpec((1, tq, MLA_HEADS * QCAT), lambda i, j: (i, j, 0)),
                  pl.BlockSpec((1, t, QCAT), lambda i, j: (i, 0, 0))],
        out_specs=pl.BlockSpec((1, tq, MLA_HEADS * KV_LORA), lambda i, j: (i, j, 0)),
        out_shape=jax.ShapeDtypeStruct((b, t, MLA_HEADS * KV_LORA), BF16),
        scratch_shapes=[pltpu.VMEM((rows, LANES), F32), pltpu.VMEM((rows, LANES), F32), pltpu.VMEM((rows, KV_LORA), F32)],
        compiler_params=_cp(("parallel", "arbitrary")),
        name="mla_prompt",
    )(q_cat, k_cat)


def _mla_sample_kernel(pt_ref, q_ref, knew_ref, ckv_hbm, krt_hbm, o_ref, ckv_buf, kr_buf, s_scr, sem,
                       *, layer, n_pages, n_seq):
    b = pl.program_id(0)
    slot = b % 2

    def start_page(seq, sl, i):
        page = pt_ref[seq * n_pages + i]
        pltpu.make_async_copy(ckv_hbm.at[layer, page], ckv_buf.at[sl, i], sem.at[sl, 0]).start()
        pltpu.make_async_copy(krt_hbm.at[layer, page], kr_buf.at[sl, i], sem.at[sl, 1]).start()

    @pl.when(b == 0)
    def _():
        def body(i, carry):
            start_page(0, 0, i)
            return carry
        lax.fori_loop(0, n_pages, body, 0)

    pltpu.make_async_copy(ckv_hbm.at[layer, pl.ds(0, n_pages)], ckv_buf.at[slot], sem.at[slot, 0]).wait()
    pltpu.make_async_copy(krt_hbm.at[layer, pl.ds(0, n_pages)], kr_buf.at[slot], sem.at[slot, 1]).wait()

    q = q_ref[0]
    ql = q[:, :KV_LORA]
    qr = q[:, KV_LORA:KV_LORA + QK_ROPE_DIM]
    n_groups = n_pages // PAGE_GROUP

    def score_pages(prefetch):
        def score_body(gi, carry):
            for i in range(PAGE_GROUP):
                pi = gi * PAGE_GROUP + i
                if prefetch:
                    start_page(b + 1, 1 - slot, pi)
                s_scr[pi] = _dot_nt(ql, _bf(ckv_buf[slot, pi])) + _dot(qr, _bf(kr_buf[slot, pi]))
            return carry
        lax.fori_loop(0, n_groups, score_body, 0)

    @pl.when(b + 1 < n_seq)
    def _():
        score_pages(True)

    @pl.when(b + 1 == n_seq)
    def _():
        score_pages(False)

    k_new = knew_ref[0].astype(F32)
    s_new = jnp.sum(q.astype(F32) * k_new, axis=1, keepdims=True)
    s = s_scr[...]
    m = jnp.maximum(jnp.max(jnp.max(s, axis=0), axis=1, keepdims=True), s_new)
    p = jnp.exp(s - m)
    p_new = jnp.exp(s_new - m)
    l = jnp.sum(jnp.sum(p, axis=0), axis=1, keepdims=True) + p_new
    s_scr[...] = p

    def pv_body(gi, acc):
        for i in range(PAGE_GROUP):
            pi = gi * PAGE_GROUP + i
            acc = acc + _dot(_bf(s_scr[pi]), _bf(ckv_buf[slot, pi]))
        return acc
    acc = lax.fori_loop(0, n_groups, pv_body, jnp.zeros((MLA_HEADS, KV_LORA), F32))
    o_ref[0] = ((acc + p_new * k_new[:, :KV_LORA]) / l).astype(o_ref.dtype)


def mla_sample(page_table_flat, q, k_new, cache_ckv, cache_krope_t, layer, n_pages):
    n_seq = q.shape[0]
    grid_spec = pltpu.PrefetchScalarGridSpec(
        num_scalar_prefetch=1,
        grid=(n_seq,),
        in_specs=[pl.BlockSpec((1, MLA_HEADS, QCAT), lambda bi, pt: (bi, 0, 0)),
                  pl.BlockSpec((1, 1, QCAT), lambda bi, pt: (bi, 0, 0)),
                  pl.BlockSpec(memory_space=pl.ANY),
                  pl.BlockSpec(memory_space=pl.ANY)],
        out_specs=pl.BlockSpec((1, MLA_HEADS, KV_LORA), lambda bi, pt: (bi, 0, 0)),
        scratch_shapes=[pltpu.VMEM((2, n_pages, PAGE_SIZE, KV_LORA), F32),
                        pltpu.VMEM((2, n_pages, QK_ROPE_DIM, PAGE_SIZE), F32),
                        pltpu.VMEM((n_pages, MLA_HEADS, PAGE_SIZE), F32),
                        pltpu.SemaphoreType.DMA((2, 2))],
    )
    return pl.pallas_call(
        functools.partial(_mla_sample_kernel, layer=layer, n_pages=n_pages, n_seq=n_seq),
        grid_spec=grid_spec,
        out_shape=jax.ShapeDtypeStruct((n_seq, MLA_HEADS, KV_LORA), BF16),
        compiler_params=_cp(("arbitrary",)),
        name="mla_sample",
    )(page_table_flat, q, k_new, cache_ckv, cache_krope_t)


def _cross_sample_kernel(q_ref, mk_ref, mv_ref, o_ref):
    scale = MEM_HEAD_DIM ** -0.5
    q = q_ref[0].astype(F32)
    q4 = jnp.concatenate([q[:, h * MEM_HEAD_DIM:(h + 1) * MEM_HEAD_DIM] for h in range(MEM_HEADS)], axis=0)
    s = jnp.sum(mk_ref[0, 0] * q4[None], axis=2, keepdims=True) * scale
    m = jnp.max(s, axis=0, keepdims=True)
    p = jnp.exp(s - m)
    l = jnp.sum(p, axis=0)
    o = jnp.sum(p * mv_ref[0, 0], axis=0) / l
    for h in range(MEM_HEADS):
        o_ref[0, :, h * MEM_HEAD_DIM:(h + 1) * MEM_HEAD_DIM] = o[h:h + 1].astype(o_ref.dtype)


def cross_sample(q, mem_k, mem_v, layer):
    b, _, d = q.shape
    blk = pl.BlockSpec((1, 1, N_MEM, MEM_HEADS, MEM_HEAD_DIM), lambda i: (layer, i, 0, 0, 0))
    return pl.pallas_call(
        _cross_sample_kernel,
        grid=(b,),
        in_specs=[pl.BlockSpec((1, 1, d), lambda i: (i, 0, 0)), blk, blk],
        out_specs=pl.BlockSpec((1, 1, d), lambda i: (i, 0, 0)),
        out_shape=jax.ShapeDtypeStruct((b, 1, d), BF16),
        compiler_params=_cp(("parallel",)),
        name="cross_sample",
    )(q, mem_k, mem_v)


def _rot_cols(w):
    half = QK_ROPE_DIM // 2
    return jnp.concatenate([-w[..., half:], w[..., :half]], axis=-1)


def _pad_lanes(w):
    return jnp.pad(w, [(0, 0)] * (w.ndim - 1) + [(0, LANES - w.shape[-1])])


def _layer_weights(l, w_in, shift_mu, rwkv_w0, rwkv_w_up, rwkv_a0, rwkv_a_up, rwkv_g_up, rwkv_k_k, rwkv_k_a,
                   rwkv_r_k, rwkv_gn_w, rwkv_gn_b, mla_q_norm, mla_w_uq, mla_kv_norm, mla_w_uk, mla_w_uv):
    w3 = RWKV_WIDTH
    o_kr = RWKV_PROJ + Q_LORA + KV_LORA
    w_kr = w_in[l][:, o_kr:]
    w_in_ext = _bf(jnp.concatenate([w_in[l][:, :o_kr], _pad_lanes(w_kr), _pad_lanes(_rot_cols(w_kr))], axis=1))
    w_lora = jnp.zeros((LORA_SLAB, 3 * w3), F32)
    w_lora = w_lora.at[:DECAY_LORA, :w3].set(rwkv_w_up[l])
    w_lora = w_lora.at[DECAY_LORA:DECAY_LORA + ICLR_LORA, w3:2 * w3].set(rwkv_a_up[l])
    w_lora = w_lora.at[DECAY_LORA + ICLR_LORA:, 2 * w3:].set(rwkv_g_up[l])
    uq = mla_w_uq[l].reshape(Q_LORA, MLA_HEADS, QK_NOPE_DIM + QK_ROPE_DIM)
    uq_nope = uq[:, :, :QK_NOPE_DIM].reshape(Q_LORA, MLA_HEADS * QK_NOPE_DIM)
    uq_rope = uq[:, :, QK_NOPE_DIM:]
    w_q = _bf(jnp.concatenate([uq_nope,
                               _pad_lanes(uq_rope).reshape(Q_LORA, MLA_HEADS * LANES),
                               _pad_lanes(_rot_cols(uq_rope)).reshape(Q_LORA, MLA_HEADS * LANES)], axis=1))
    eye_h = jnp.eye(MLA_HEADS, dtype=F32)
    wuk_bd = _bf(jnp.einsum('chd,hg->hdgc', mla_w_uk[l], eye_h).reshape(MLA_HEADS * QK_NOPE_DIM, MLA_HEADS * KV_LORA))
    wuv_bd = _bf(jnp.einsum('chd,hg->hcgd', mla_w_uv[l], eye_h).reshape(MLA_HEADS * KV_LORA, MLA_HEADS * V_HEAD_DIM))
    row = lambda a: a.reshape(1, -1)
    return dict(
        w_in_ext=w_in_ext, mu=row(shift_mu[l]), w0=row(rwkv_w0[l]), a0=row(rwkv_a0[l]), w_lora=_bf(w_lora),
        k_k=row(rwkv_k_k[l]), k_a=row(rwkv_k_a[l]), rk=row(rwkv_r_k[l]), gnw=row(rwkv_gn_w[l]), gnb=row(rwkv_gn_b[l]),
        q_norm=row(mla_q_norm[l]), kv_norm=row(mla_kv_norm[l]), w_q=w_q, wuk_bd=wuk_bd, wuv_bd=wuv_bd)


def _rope_tables(pos):
    half = QK_ROPE_DIM // 2
    inv_freq = ROPE_THETA ** (-jnp.arange(half, dtype=F32) / half)
    ang = pos.astype(F32)[:, None] * inv_freq[None, :]
    cos, sin = jnp.cos(ang), jnp.sin(ang)
    return (_pad_lanes(jnp.concatenate([cos, cos], axis=1)), _pad_lanes(jnp.concatenate([sin, sin], axis=1)))


def kernel(x_prompt, x_sample, cache_ckv, cache_krope, state_wkv, state_shift, cache_mem_k, cache_mem_v, page_table, mem_prompt, norm_ffn1, ffn1_w_gate, ffn1_w_up, ffn1_w_down, norm_mix, w_in, shift_mu, rwkv_w0, rwkv_w_up, rwkv_a0, rwkv_a_up, rwkv_g_up, rwkv_k_k, rwkv_k_a, rwkv_r_k, rwkv_gn_w, rwkv_gn_b, mla_q_norm, mla_w_uq, mla_kv_norm, mla_w_uk, mla_w_uv, w_out, norm_cross, norm_mem, mem_w_q, mem_w_k, mem_w_v, mem_w_o, norm_ffn2, ffn2_w_gate, ffn2_w_up, ffn2_w_down, norm_final):
    bp, tp, d = x_prompt.shape
    bs, ts, _ = x_sample.shape
    assert ts == 1
    n_pages = page_table.shape[1]
    assert n_pages % PAGE_GROUP == 0
    past_len = n_pages * PAGE_SIZE
    hh, nn = RWKV_HEADS, RWKV_HEAD_DIM
    row = lambda a: a.reshape(1, -1)

    head_of = jnp.arange(RWKV_WIDTH) // nn
    ones_bd = (head_of[:, None] == head_of[None, :]).astype(BF16)
    cos_p, sin_p = _rope_tables(jnp.arange(tp))
    cos_s, sin_s = _rope_tables(jnp.full((bs,), past_len, jnp.int32))
    pt_flat = page_table.reshape(-1)
    krope_t = jnp.swapaxes(cache_krope, 2, 3)

    lw = [_layer_weights(l, w_in, shift_mu, rwkv_w0, rwkv_w_up, rwkv_a0, rwkv_a_up, rwkv_g_up, rwkv_k_k, rwkv_k_a,
                         rwkv_r_k, rwkv_gn_w, rwkv_gn_b, mla_q_norm, mla_w_uq, mla_kv_norm, mla_w_uk, mla_w_uv)
          for l in range(DEPTH)]

    tm_p, tm_s = 512, bs

    def run_ffn(x2, l, which, tm, final):
        if which == 1:
            g, wg, wu, wd = norm_ffn1[l], ffn1_w_gate[l], ffn1_w_up[l], ffn1_w_down[l]
        else:
            g, wg, wu, wd = norm_ffn2[l], ffn2_w_gate[l], ffn2_w_up[l], ffn2_w_down[l]
        return ffn(x2, row(g), _bf(wg), _bf(wu), _bf(wd), row(norm_final), final, tm)

    mem2 = mem_prompt.reshape(bp * N_MEM, d)
    mem_k_p, mem_v_p = [], []
    for l in range(DEPTH):
        kv = norm_matmul(mem2, row(norm_mem[l]), _bf(jnp.concatenate([mem_w_k[l], mem_w_v[l]], axis=1)), F32, 512)
        mem_k_p.append(kv[:, :d].reshape(bp, N_MEM, d))
        mem_v_p.append(kv[:, d:].reshape(bp, N_MEM, d))

    def trunk(x, sample):
        b, t, _ = x.shape
        m = b * t
        tm = tm_s if sample else tm_p
        x2 = x.reshape(m, d)
        ckvs, krs, wkvs, shifts = [], [], [], []
        for l in range(DEPTH):
            p = lw[l]
            x2 = run_ffn(x2, l, 1, tm, False)
            if sample:
                outs = prep(x2.reshape(1, m, d), row(norm_mix[l]), p['w_in_ext'],
                            state_shift[l].reshape(1, m, RWKV_PROJ), True,
                            p['mu'], p['w0'], p['a0'], p['w_lora'], p['k_k'], p['k_a'], ones_bd, p['q_norm'],
                            p['kv_norm'], p['w_q'], p['wuk_bd'], cos_s, sin_s, m)
            else:
                outs = prep(x2.reshape(b, t, d), row(norm_mix[l]), p['w_in_ext'],
                            jnp.zeros((b, 1, RWKV_PROJ), F32), False,
                            p['mu'], p['w0'], p['a0'], p['w_lora'], p['k_k'], p['k_a'], ones_bd, p['q_norm'],
                            p['kv_norm'], p['w_q'], p['wuk_bd'], cos_p, sin_p, 256)
            shift_last = outs[11].reshape(b, RWKV_PROJ)
            r, k, v, kk, bv, lwd, g, ckv, kr, q_cat, k_cat = outs[:11]
            if sample:
                to_heads = lambda a: a.reshape(m, hh, nn)
                o_rwkv, wkv = rwkv_step(to_heads(r), to_heads(k), to_heads(v), to_heads(kk), to_heads(bv),
                                        to_heads(lwd), to_heads(g), p['rk'].reshape(hh, nn), p['gnw'].reshape(hh, nn),
                                        p['gnb'].reshape(hh, nn), state_wkv[l], 8)
                o_rwkv = _bf(o_rwkv.reshape(m, RWKV_WIDTH))
                o_lat = mla_sample(pt_flat, q_cat.reshape(m, MLA_HEADS, QCAT), k_cat.reshape(m, 1, QCAT),
                                   cache_ckv, krope_t, l, n_pages).reshape(m, MLA_HEADS * KV_LORA)
                x2 = mix_out(o_rwkv, o_lat, p['wuv_bd'], _bf(w_out[l]), x2, tm)
                q = norm_matmul(x2, row(norm_cross[l]), _bf(mem_w_q[l]), F32, tm)
                att = cross_sample(q.reshape(b, t, d), cache_mem_k, cache_mem_v, l)
                x2 = matmul_res(att.reshape(m, d), _bf(mem_w_o[l]), x2, tm)
            else:
                o_rwkv, s_bd = rwkv_chunked(r, k, v, kk, bv, lwd, g, p['rk'], p['gnw'], p['gnb'], CHUNK, CHUNK_SEQS)
                wkv = jnp.stack([s_bd[:, :, :nn, :nn], s_bd[:, :, nn:, nn:]], axis=2).reshape(b, hh, nn, nn)
                o_lat = mla_prompt(q_cat, k_cat, 256)
                x2 = post_mix(o_rwkv, o_lat, x2.reshape(b, t, d), p['wuv_bd'], _bf(w_out[l]), row(norm_cross[l]),
                              _bf(mem_w_q[l]), mem_k_p[l], mem_v_p[l], _bf(mem_w_o[l]), 512).reshape(m, d)
            x2 = run_ffn(x2, l, 2, tm, l == DEPTH - 1)
            ckvs.append(ckv.reshape(b, t, KV_LORA))
            krs.append(kr.reshape(b, t, QK_ROPE_DIM))
            wkvs.append(wkv)
            shifts.append(shift_last)
        return x2.reshape(b, t, d), jnp.stack(ckvs), jnp.stack(krs), jnp.stack(wkvs), jnp.stack(shifts)

    y_p, ckv_p, kr_p, wkv_p, shift_p = trunk(x_prompt, False)
    y_s, ckv_s, kr_s, wkv_s, shift_s = trunk(x_sample, True)
    mem_shape = (DEPTH, bp, N_MEM, MEM_HEADS, MEM_HEAD_DIM)
    return (y_p, y_s, ckv_p, kr_p, wkv_p, shift_p,
            jnp.stack(mem_k_p).reshape(mem_shape), jnp.stack(mem_v_p).reshape(mem_shape),
            ckv_s, kr_s, wkv_s, shift_s)
```

```python
import functools

import jax
import jax.numpy as jnp
from jax import lax
from jax.experimental import pallas as pl
from jax.experimental.pallas import tpu as pltpu

F32 = jnp.float32
BF16 = jnp.bfloat16

D_MODEL = 1024
DEPTH = 2
PAGE_SIZE = 128
RWKV_HEADS = 8
RWKV_HEAD_DIM = 64
RWKV_WIDTH = 512
DECAY_LORA = 64
ICLR_LORA = 64
GATE_LORA = 128
GN_EPS = 64e-5
MLA_HEADS = 8
QK_NOPE_DIM = 64
QK_ROPE_DIM = 32
V_HEAD_DIM = 64
Q_LORA = 256
KV_LORA = 128
ROPE_THETA = 10000.0
MLA_SCALE = (QK_NOPE_DIM + QK_ROPE_DIM) ** -0.5
RWKV_PROJ = 3 * RWKV_WIDTH + DECAY_LORA + ICLR_LORA + GATE_LORA
N_MEM = 256
MEM_HEADS = 4
MEM_HEAD_DIM = D_MODEL // MEM_HEADS
D_FF = 2816
RMS_EPS = 1e-6

LANES = 128
IN_PROJ_EXT = RWKV_PROJ + Q_LORA + KV_LORA + 2 * LANES
LORA_SLAB = DECAY_LORA + ICLR_LORA + GATE_LORA
QCAT = 2 * LANES
CHUNK = 64
CHUNK_SEQS = 4
FFN_CHUNK = 256
NEG = -1e30
VMEM_LIMIT = 56 * 1024 * 1024


def _cp(sem):
    return pltpu.CompilerParams(dimension_semantics=sem, vmem_limit_bytes=VMEM_LIMIT)


def _dot(a, b):
    return jnp.dot(a, b, preferred_element_type=F32)


def _dot_nt(a, b):
    return lax.dot_general(a, b, (((1,), (1,)), ((), ())), preferred_element_type=F32)


def _bf(x):
    return x.astype(BF16)


def _split3(x):
    hi = _bf(x)
    r1 = x - hi.astype(F32)
    mid = _bf(r1)
    return hi, mid, _bf(r1 - mid.astype(F32))


def _dot_split2(x, w):
    hi = _bf(x)
    return _dot(hi, w) + _dot(_bf(x - hi.astype(F32)), w)


def _rms(x, g):
    return x * lax.rsqrt(jnp.mean(x * x, axis=-1, keepdims=True) + RMS_EPS) * g


def _sigmoid(x):
    return 1.0 / (1.0 + jnp.exp(-x))


def _norm_matmul_kernel(x_ref, g_ref, w_ref, o_ref):
    h = _bf(_rms(x_ref[...], g_ref[...]))
    o_ref[...] = _dot(h, w_ref[...]).astype(o_ref.dtype)


def norm_matmul(x, g, w, out_dtype, tm):
    m, d = x.shape
    n = w.shape[1]
    return pl.pallas_call(
        _norm_matmul_kernel,
        grid=(m // tm,),
        in_specs=[pl.BlockSpec((tm, d), lambda i: (i, 0)),
                  pl.BlockSpec((1, d), lambda i: (0, 0)),
                  pl.BlockSpec((d, n), lambda i: (0, 0))],
        out_specs=pl.BlockSpec((tm, n), lambda i: (i, 0)),
        out_shape=jax.ShapeDtypeStruct((m, n), out_dtype),
        compiler_params=_cp(("parallel",)),
        name="norm_matmul",
    )(x, g, w)


def _memory_kv_kernel(m_ref, g_ref, wk_ref, wv_ref, k_o, v_o):
    h = _bf(_rms(m_ref[...], g_ref[...]))
    k_o[...] = _dot(h, _bf(wk_ref[...]))
    v_o[...] = _dot(h, _bf(wv_ref[...]))


def memory_kv(mem, g, w_k, w_v, tm):
    rows, d = mem.shape
    depth = g.shape[0]
    per_layer = lambda shape: pl.BlockSpec((None,) + shape, lambda l, i: (l, 0, 0))
    out = pl.BlockSpec((None, tm, d), lambda l, i: (l, i, 0))
    return pl.pallas_call(
        _memory_kv_kernel,
        grid=(depth, rows // tm),
        in_specs=[pl.BlockSpec((tm, d), lambda l, i: (i, 0)), per_layer((1, d)), per_layer((d, d)), per_layer((d, d))],
        out_specs=[out, out],
        out_shape=[jax.ShapeDtypeStruct((depth, rows, d), F32)] * 2,
        compiler_params=_cp(("parallel", "parallel")),
        name="memory_kv",
    )(mem, g, w_k, w_v)


def _ffn_kernel(x_ref, g_ref, wg_ref, wu_ref, wd_ref, gf_ref, o_ref, act_scr, *, final_norm):
    x = x_ref[...]
    h = _bf(_rms(x, g_ref[...]))
    dff = wg_ref.shape[1]
    for c0 in range(0, dff, FFN_CHUNK):
        cs = slice(c0, min(c0 + FFN_CHUNK, dff))
        gate = _dot(h, _bf(wg_ref[:, cs]))
        up = _dot(h, _bf(wu_ref[:, cs]))
        act_scr[:, cs] = _bf(gate * _sigmoid(gate) * up)
    y = x + 0.5 * _dot(act_scr[...], _bf(wd_ref[...]))
    if final_norm:
        y = _rms(y, gf_ref[...])
    o_ref[...] = y


def _resident(shape):
    return pl.BlockSpec(shape, lambda *_: (0,) * len(shape), pipeline_mode=pl.Buffered(1))


def _layer_resident(arr, layer):
    return pl.BlockSpec((None,) + arr.shape[1:], lambda *_: (layer,) + (0,) * (arr.ndim - 1),
                        pipeline_mode=pl.Buffered(1))


def ffn(x, g, wg, wu, wd, layer, gf, final_norm, tm):
    m, d = x.shape
    dff = wg.shape[2]
    return pl.pallas_call(
        functools.partial(_ffn_kernel, final_norm=final_norm),
        grid=(m // tm,),
        in_specs=[pl.BlockSpec((tm, d), lambda i: (i, 0)), _resident((1, d)), _layer_resident(wg, layer),
                  _layer_resident(wu, layer), _layer_resident(wd, layer), _resident((1, d))],
        out_specs=pl.BlockSpec((tm, d), lambda i: (i, 0)),
        out_shape=jax.ShapeDtypeStruct((m, d), F32),
        scratch_shapes=[pltpu.VMEM((tm, dff), BF16)],
        compiler_params=_cp(("parallel",)),
        name="ffn",
    )(x, g, wg, wu, wd, gf)


def _matmul_res_kernel(a_ref, w_ref, x_ref, o_ref):
    o_ref[...] = x_ref[...] + _dot(a_ref[...], w_ref[...])


def matmul_res(a, w, x, tm):
    m, k = a.shape
    n = w.shape[1]
    return pl.pallas_call(
        _matmul_res_kernel,
        grid=(m // tm,),
        in_specs=[pl.BlockSpec((tm, k), lambda i: (i, 0)),
                  pl.BlockSpec((k, n), lambda i: (0, 0)),
                  pl.BlockSpec((tm, n), lambda i: (i, 0))],
        out_specs=pl.BlockSpec((tm, n), lambda i: (i, 0)),
        out_shape=jax.ShapeDtypeStruct((m, n), F32),
        compiler_params=_cp(("parallel",)),
        name="matmul_res",
    )(a, w, x)


def _mix_out_kernel(orw_ref, olat_ref, wuv_ref, wout_ref, x_ref, o_ref):
    o_mla = _bf(_dot(olat_ref[...], wuv_ref[...]))
    y = _dot(orw_ref[...], wout_ref[:RWKV_WIDTH, :]) + _dot(o_mla, wout_ref[RWKV_WIDTH:, :])
    o_ref[...] = x_ref[...] + y


def mix_out(o_rwkv, o_lat, wuv_bd, w_out, x, tm):
    m, d = x.shape
    return pl.pallas_call(
        _mix_out_kernel,
        grid=(m // tm,),
        in_specs=[pl.BlockSpec((tm, RWKV_WIDTH), lambda i: (i, 0)),
                  pl.BlockSpec((tm, MLA_HEADS * KV_LORA), lambda i: (i, 0)),
                  pl.BlockSpec(wuv_bd.shape, lambda i: (0, 0)),
                  pl.BlockSpec(w_out.shape, lambda i: (0, 0)),
                  pl.BlockSpec((tm, d), lambda i: (i, 0))],
        out_specs=pl.BlockSpec((tm, d), lambda i: (i, 0)),
        out_shape=jax.ShapeDtypeStruct((m, d), F32),
        compiler_params=_cp(("parallel",)),
        name="mix_out",
    )(o_rwkv, o_lat, wuv_bd, w_out, x)


def _post_mix_kernel(orw_ref, olat_ref, x_ref, wuv_ref, wout_ref, gc_ref, wq_ref, mk_ref, mv_ref, wo_ref, o_ref):
    o_mla = _bf(_dot(olat_ref[0], wuv_ref[...]))
    x1 = x_ref[0] + _dot(orw_ref[0], wout_ref[:RWKV_WIDTH, :]) + _dot(o_mla, wout_ref[RWKV_WIDTH:, :])
    q = _bf(_dot(_bf(_rms(x1, gc_ref[...])), wq_ref[...]))
    scale = MEM_HEAD_DIM ** -0.5
    att = []
    for h in range(MEM_HEADS):
        hs = slice(h * MEM_HEAD_DIM, (h + 1) * MEM_HEAD_DIM)
        s = _dot_nt(q[:, hs], _bf(mk_ref[0, :, hs])) * scale
        p = jnp.exp(s - jnp.max(s, axis=1, keepdims=True))
        l = jnp.sum(p, axis=1, keepdims=True)
        att.append(_bf(_dot(_bf(p), _bf(mv_ref[0, :, hs])) / l))
    o_ref[0] = x1 + _dot(jnp.concatenate(att, axis=1), wo_ref[...])


def post_mix(o_rwkv, o_lat, x, wuv_bd, w_out, g_cross, w_q, mem_k, mem_v, layer, w_o, tq):
    b, t, d = x.shape
    seq = lambda w: pl.BlockSpec((1, tq, w), lambda i, j: (i, j, 0))
    mem = pl.BlockSpec((None, 1, N_MEM, d), lambda i, j: (layer, i, 0, 0))
    return pl.pallas_call(
        _post_mix_kernel,
        grid=(b, t // tq),
        in_specs=[seq(RWKV_WIDTH), seq(MLA_HEADS * KV_LORA), seq(d), _resident(wuv_bd.shape), _resident(w_out.shape),
                  _resident(g_cross.shape), _resident(w_q.shape), mem, mem, _resident(w_o.shape)],
        out_specs=seq(d),
        out_shape=jax.ShapeDtypeStruct((b, t, d), F32),
        compiler_params=_cp(("parallel", "arbitrary")),
        name="post_mix",
    )(o_rwkv, o_lat, x, wuv_bd, w_out, g_cross, w_q, mem_k, mem_v, w_o)


def _prep_kernel(x_ref, gmix_ref, win_ref, prev_ref, mu_ref, w0_ref, a0_ref, wl_ref, kk_ref, ka_ref, ones_ref,
                 qn_ref, kvn_ref, wq_ref, wuk_ref, cos_ref, sin_ref,
                 r_o, k_o, v_o, kkn_o, b_o, lw_o, g_o, ckv_o, kr_o, qcat_o, kcat_o, raw_o,
                 proj_ref, carry, *, tm, explicit_prev):
    proj_ref[...] = _dot(_bf(_rms(x_ref[0], gmix_ref[...])), win_ref[...])
    p = proj_ref[:, :RWKV_PROJ]
    if explicit_prev:
        prev = prev_ref[0]
        raw_o[0] = p
    else:
        @pl.when(pl.program_id(1) == 0)
        def _():
            carry[...] = prev_ref[0]

        row = lax.broadcasted_iota(jnp.int32, (tm, 1), 0)
        prev = jnp.where(row == 0, carry[...], pltpu.roll(p, 1, axis=0))
        carry[...] = p[tm - 1:tm, :]
        raw_o[0] = p[tm - 1:tm, :]
    ps = p + (prev - p) * mu_ref[...]

    w3 = RWKV_WIDTH
    r = ps[:, :w3]
    k = ps[:, w3:2 * w3]
    v = ps[:, 2 * w3:3 * w3]
    slab = ps[:, 3 * w3:]
    lane = lax.broadcasted_iota(jnp.int32, slab.shape, 1)
    act = jnp.where(lane < DECAY_LORA, jnp.tanh(slab),
                    jnp.where(lane < DECAY_LORA + ICLR_LORA, slab, _sigmoid(slab)))
    lo = _dot_split2(act, wl_ref[...])
    z = -(w0_ref[...] + lo[:, :w3])
    softplus = jnp.maximum(z, 0.0) + jnp.log(1.0 + jnp.exp(-jnp.abs(z)))
    lw = -jnp.exp(-softplus - 0.5)
    a = _sigmoid(a0_ref[...] + lo[:, w3:2 * w3])
    g = lo[:, 2 * w3:]
    kkr = k * kk_ref[...]
    ssq = _dot_split2(kkr * kkr, ones_ref[...])
    kkn = kkr / jnp.maximum(jnp.sqrt(ssq), 1e-12)
    r_o[0] = r
    k_o[0] = k * (1.0 + (a - 1.0) * ka_ref[...])
    v_o[0] = v
    kkn_o[0] = kkn
    b_o[0] = kkn * a
    lw_o[0] = lw
    g_o[0] = g

    o1 = RWKV_PROJ
    cq = _bf(_rms(proj_ref[:, o1:o1 + Q_LORA], qn_ref[...]))
    o2 = o1 + Q_LORA
    ckv = _rms(proj_ref[:, o2:o2 + KV_LORA], kvn_ref[...])
    o3 = o2 + KV_LORA
    cos = cos_ref[...]
    sin = sin_ref[...]
    kr = proj_ref[:, o3:o3 + LANES] * cos + proj_ref[:, o3 + LANES:o3 + 2 * LANES] * sin
    ckv_o[0] = ckv
    kr_o[0] = kr[:, :QK_ROPE_DIM]
    kcat_o[0] = _bf(jnp.concatenate([ckv, kr], axis=1))

    qq = _dot(cq, wq_ref[...])
    nn = MLA_HEADS * QK_NOPE_DIM
    nr = MLA_HEADS * LANES
    q_lat = _dot(_bf(qq[:, :nn]), wuk_ref[...]) * MLA_SCALE
    cos8 = jnp.concatenate([cos] * MLA_HEADS, axis=1)
    sin8 = jnp.concatenate([sin] * MLA_HEADS, axis=1)
    q_rp = (qq[:, nn:nn + nr] * cos8 + qq[:, nn + nr:] * sin8) * MLA_SCALE
    pieces = []
    for h in range(MLA_HEADS):
        pieces.append(q_lat[:, h * LANES:(h + 1) * LANES])
        pieces.append(q_rp[:, h * LANES:(h + 1) * LANES])
    qcat_o[0] = _bf(jnp.concatenate(pieces, axis=1))


def prep(x, g_mix, w_in_ext, prev, explicit_prev, mu, w0, a0, w_lora, k_k, k_a, ones_bd, q_norm, kv_norm, w_q, wuk_bd,
         cos_t, sin_t, tm):
    b, t, d = x.shape
    nt = t // tm
    wide = lambda w: pl.BlockSpec((1, tm, w), lambda i, j: (i, j, 0))
    full = lambda arr: _resident(arr.shape)
    if explicit_prev:
        prev_spec = wide(RWKV_PROJ)
        raw_spec, raw_rows = wide(RWKV_PROJ), t
    else:
        prev_spec = pl.BlockSpec((1, 1, RWKV_PROJ), lambda i, j: (i, 0, 0))
        raw_spec, raw_rows = pl.BlockSpec((1, 1, RWKV_PROJ), lambda i, j: (i, 0, 0)), 1
    sd = lambda w, dt: jax.ShapeDtypeStruct((b, t, w), dt)
    w3 = RWKV_WIDTH
    out_shape = [sd(w3, F32)] * 7 + [sd(KV_LORA, F32), sd(QK_ROPE_DIM, F32),
                                      sd(MLA_HEADS * QCAT, BF16), sd(QCAT, BF16),
                                      jax.ShapeDtypeStruct((b, raw_rows, RWKV_PROJ), F32)]
    out_specs = [wide(w3)] * 7 + [wide(KV_LORA), wide(QK_ROPE_DIM), wide(MLA_HEADS * QCAT), wide(QCAT), raw_spec]
    return pl.pallas_call(
        functools.partial(_prep_kernel, tm=tm, explicit_prev=explicit_prev),
        grid=(b, nt),
        in_specs=[wide(d), full(g_mix), full(w_in_ext), prev_spec, full(mu), full(w0), full(a0), full(w_lora),
                  full(k_k), full(k_a), full(ones_bd), full(q_norm), full(kv_norm), full(w_q), full(wuk_bd),
                  pl.BlockSpec((tm, LANES), lambda i, j: (j, 0)),
                  pl.BlockSpec((tm, LANES), lambda i, j: (j, 0))],
        out_specs=out_specs,
        out_shape=out_shape,
        scratch_shapes=[pltpu.VMEM((tm, IN_PROJ_EXT), F32), pltpu.VMEM((1, RWKV_PROJ), F32)],
        compiler_params=_cp(("arbitrary", "arbitrary")),
        name="prep",
    )(x, g_mix, w_in_ext, prev, mu, w0, a0, w_lora, k_k, k_a, ones_bd, q_norm, kv_norm, w_q, wuk_bd, cos_t, sin_t)


def _rwkv_chunk_kernel(r_ref, k_ref, v_ref, kk_ref, b_ref, lw_ref, g_ref, rk_ref, gnw_ref, gnb_ref,
                       o_ref, s_out_ref, s_scr, *, c, n_chunks, nb):
    @pl.when(pl.program_id(1) == 0)
    def _():
        s_scr[...] = jnp.zeros_like(s_scr)

    n = RWKV_HEAD_DIM
    npair = RWKV_HEADS // 2
    row = lax.broadcasted_iota(jnp.int32, (c, c), 0)
    col = lax.broadcasted_iota(jnp.int32, (c, c), 1)
    incl = row >= col
    strict = row > col
    tri = _bf(jnp.where(incl, 1.0, 0.0))
    eye = jnp.where(row == col, 1.0, 0.0).astype(F32)
    lane = lax.broadcasted_iota(jnp.int32, (c, LANES), 1)
    is_e = lane < n
    r2 = lax.broadcasted_iota(jnp.int32, (LANES, LANES), 0)
    c2 = lax.broadcasted_iota(jnp.int32, (LANES, LANES), 1)
    same_head = (r2 < n) == (c2 < n)
    ones_bd = _bf(jnp.where(same_head, 1.0, 0.0))
    n_double = c.bit_length() - 2
    pairs = range(nb * npair)
    heads = [(j, x) for j in pairs for x in range(2)]
    sl = [slice((j % npair) * LANES, (j % npair + 1) * LANES) for j in pairs]
    bi = [j // npair for j in pairs]

    r = [r_ref[bi[j], :, sl[j]] for j in pairs]
    k = [k_ref[bi[j], :, sl[j]] for j in pairs]
    v = [v_ref[bi[j], :, sl[j]] for j in pairs]
    kk = [kk_ref[bi[j], :, sl[j]] for j in pairs]
    bv = [b_ref[bi[j], :, sl[j]] for j in pairs]
    lw = [lw_ref[bi[j], :, sl[j]] for j in pairs]
    s0 = [s_scr[bi[j], j % npair] for j in pairs]

    cum = [sum(_dot(tri, part) for part in _split3(lw[j])) for j in pairs]
    e_pos = [jnp.exp(cum[j]) for j in pairs]
    e_neg = [jnp.exp(-cum[j]) for j in pairs]
    a_t = [-kk[j] * jnp.exp(cum[j] - lw[j]) for j in pairs]
    r_t = [r[j] * e_pos[j] for j in pairs]
    b_t = [bv[j] * e_neg[j] for j in pairs]
    k_t = [k[j] * e_neg[j] for j in pairs]
    g_c = [e_pos[j][c - 1:c, :] for j in pairs]

    v_b = [_bf(v[j]) for j in pairs]
    s0_b = [_bf(s0[j]) for j in pairs]
    zero = jnp.zeros((c, LANES), F32)
    xb, xk = [], []
    for j in pairs:
        lhs = _bf(jnp.concatenate([jnp.where(is_e, a_t[j], zero), jnp.where(is_e, r_t[j], zero),
                                   jnp.where(is_e, zero, a_t[j]), jnp.where(is_e, zero, r_t[j])], axis=0))
        xb.append(_dot_nt(lhs, _bf(b_t[j])))
        xk.append(_dot_nt(lhs, _bf(k_t[j])))
    rs0 = [_dot_nt(_bf(r_t[j]), s0_b[j]) for j in pairs]

    nmat, arb, aakv, o2 = {}, {}, {}, {}
    for (j, x) in heads:
        base = 2 * c * x
        nmat[j, x] = jnp.where(strict, xb[j][base:base + c], 0.0)
        arb[j, x] = _bf(jnp.where(incl, xb[j][base + c:base + 2 * c], 0.0))
        aakv[j, x] = _dot(_bf(jnp.where(strict, xk[j][base:base + c], 0.0)), v_b[j])
        o2[j, x] = _dot(_bf(jnp.where(incl, xk[j][base + c:base + 2 * c], 0.0)), v_b[j])

    tinv = {h: eye + nmat[h] for h in heads}
    pw = dict(nmat)
    for _ in range(n_double):
        pw = {h: _dot(_bf(pw[h]), _bf(pw[h])) for h in heads}
        tinv = {h: tinv[h] + _dot(_bf(pw[h]), _bf(tinv[h])) for h in heads}

    w12 = {h: _dot(_bf(tinv[h]), _bf(jnp.concatenate([a_t[h[0]], aakv[h]], axis=1))) for h in heads}
    u = []
    for j in pairs:
        w1 = jnp.where(is_e, w12[j, 0][:, :LANES], w12[j, 1][:, :LANES])
        w2 = jnp.where(is_e, w12[j, 0][:, LANES:], w12[j, 1][:, LANES:])
        u.append(_dot_nt(_bf(w1), s0_b[j]) + w2)
    u_b = [_bf(u[j]) for j in pairs]
    o = [rs0[j] + jnp.where(is_e, _dot(arb[j, 0], u_b[j]) + o2[j, 0], _dot(arb[j, 1], u_b[j]) + o2[j, 1])
         for j in pairs]
    for j in pairs:
        uv = jnp.concatenate([u[j], v[j]], axis=0)
        bk = jnp.concatenate([b_t[j] * g_c[j], k_t[j] * g_c[j]], axis=0)
        s_scr[bi[j], j % npair] = s0[j] * g_c[j] + jnp.where(same_head, _dot(_bf(uv.T), _bf(bk)), 0.0)

    inv_n = 1.0 / n
    o_all = jnp.concatenate(o, axis=0)
    mean = _dot(_bf(o_all), ones_bd) * inv_n
    d = o_all - mean
    var = _dot(_bf(d * d), ones_bd) * inv_n
    bsum = _dot(_bf(jnp.concatenate([r[j] * k[j] * rk_ref[:, sl[j]] for j in pairs], axis=0)), ones_bd)
    on = d * lax.rsqrt(var + GN_EPS)
    for j in pairs:
        rows = slice(j * c, (j + 1) * c)
        out = (on[rows] * gnw_ref[:, sl[j]] + gnb_ref[:, sl[j]] + bsum[rows] * v[j]) * g_ref[bi[j], :, sl[j]]
        o_ref[bi[j], :, sl[j]] = out.astype(o_ref.dtype)

    @pl.when(pl.program_id(1) == n_chunks - 1)
    def _():
        s_out_ref[...] = s_scr[...]


def rwkv_chunked(r, k, v, kk, bv, lw, g, rk, gnw, gnb, c, nb):
    b, t, w = r.shape
    n_chunks = t // c
    seq = pl.BlockSpec((nb, c, w), lambda i, j: (i, j, 0))
    par = pl.BlockSpec((1, w), lambda i, j: (0, 0))
    npair = RWKV_HEADS // 2
    return pl.pallas_call(
        functools.partial(_rwkv_chunk_kernel, c=c, n_chunks=n_chunks, nb=nb),
        grid=(b // nb, n_chunks),
        in_specs=[seq] * 7 + [par] * 3,
        out_specs=[seq, pl.BlockSpec((nb, npair, LANES, LANES), lambda i, j: (i, 0, 0, 0))],
        out_shape=[jax.ShapeDtypeStruct((b, t, w), BF16),
                   jax.ShapeDtypeStruct((b, npair, LANES, LANES), F32)],
        scratch_shapes=[pltpu.VMEM((nb, npair, LANES, LANES), F32)],
        compiler_params=_cp(("arbitrary", "arbitrary")),
        name="rwkv_chunked",
    )(r, k, v, kk, bv, lw, g, rk, gnw, gnb)


def _rwkv_step_kernel(r_ref, k_ref, v_ref, kk_ref, b_ref, lw_ref, g_ref, rk_ref, gnw_ref, gnb_ref, s_ref,
                      o_ref, s_out_ref, *, bb):
    n = RWKV_HEAD_DIM
    eye = lax.broadcasted_iota(jnp.int32, (n, n), 0) == lax.broadcasted_iota(jnp.int32, (n, n), 1)

    def body(i, carry):
        for h in range(RWKV_HEADS):
            hs = slice(h, h + 1)
            s = s_ref[i, h]
            r = r_ref[i, hs, :]
            k = k_ref[i, hs, :]
            v = v_ref[i, hs, :]
            kk = kk_ref[i, hs, :]
            bv = b_ref[i, hs, :]
            w = jnp.exp(lw_ref[i, hs, :])
            s_kk = jnp.sum(s * kk, axis=1, keepdims=True)
            v_col = jnp.sum(jnp.where(eye, v, 0.0), axis=1, keepdims=True)
            s_new = s * w - s_kk * bv + v_col * k
            o_col = jnp.sum(s_new * r, axis=1, keepdims=True)
            o = jnp.sum(jnp.where(eye, o_col, 0.0), axis=0, keepdims=True)
            mean = jnp.mean(o, axis=1, keepdims=True)
            d = o - mean
            var = jnp.mean(d * d, axis=1, keepdims=True)
            on = d * lax.rsqrt(var + GN_EPS) * gnw_ref[hs, :] + gnb_ref[hs, :]
            bonus = jnp.sum(r * k * rk_ref[hs, :], axis=1, keepdims=True) * v
            o_ref[i, hs, :] = ((on + bonus) * g_ref[i, hs, :]).astype(o_ref.dtype)
            s_out_ref[i, h] = s_new
        return carry

    lax.fori_loop(0, bb, body, 0)


def rwkv_step(r, k, v, kk, bv, lw, g, rk, gnw, gnb, s_all, layer, bb):
    b = r.shape[0]
    h, n = RWKV_HEADS, RWKV_HEAD_DIM
    vec = pl.BlockSpec((bb, h, n), lambda i: (i, 0, 0))
    par = pl.BlockSpec((h, n), lambda i: (0, 0))
    st = pl.BlockSpec((bb, h, n, n), lambda i: (i, 0, 0, 0))
    return pl.pallas_call(
        functools.partial(_rwkv_step_kernel, bb=bb),
        grid=(b // bb,),
        in_specs=[vec] * 7 + [par] * 3 + [pl.BlockSpec((None, bb, h, n, n), lambda i: (layer, i, 0, 0, 0))],
        out_specs=[vec, st],
        out_shape=[jax.ShapeDtypeStruct((b, h, n), F32), jax.ShapeDtypeStruct((b, h, n, n), F32)],
        compiler_params=_cp(("parallel",)),
        name="rwkv_step",
    )(r, k, v, kk, bv, lw, g, rk, gnw, gnb, s_all)


def _mla_prompt_kernel(q_ref, k_ref, o_ref, m_scr, l_scr, acc_scr, *, t):
    i = pl.program_id(1)
    nh = MLA_HEADS
    q = jnp.concatenate([q_ref[0, :, h * QCAT:(h + 1) * QCAT] for h in range(nh)], axis=0)
    row = lax.broadcasted_iota(jnp.int32, (t, 1), 0)
    causal = lax.broadcasted_iota(jnp.int32, (1, t), 1) <= jnp.concatenate([row] * nh, axis=0)
    ones = jnp.ones((t, LANES), BF16)
    m_scr[...] = jnp.full_like(m_scr, NEG)
    l_scr[...] = jnp.zeros_like(l_scr)
    acc_scr[...] = jnp.zeros_like(acc_scr)

    def step(j, masked):
        kb = k_ref[0, pl.ds(pl.multiple_of(j * t, t), t), :]
        s = _dot_nt(q, kb)
        if masked:
            s = jnp.where(causal, s, NEG)
        m_old = m_scr[...]
        m_new = jnp.maximum(m_old, jnp.max(s, axis=1, keepdims=True))
        p = jnp.exp(s - jnp.concatenate([m_new] * (t // LANES), axis=1))
        alpha = jnp.exp(m_old - m_new)
        pv = _dot(_bf(p), jnp.concatenate([kb[:, :KV_LORA], ones], axis=1))
        acc_scr[...] = alpha * acc_scr[...] + pv[:, :KV_LORA]
        l_scr[...] = alpha * l_scr[...] + pv[:, KV_LORA:]
        m_scr[...] = m_new

    def body(j, carry):
        step(j, False)
        return carry

    lax.fori_loop(0, i, body, 0)
    step(i, True)
    o = acc_scr[...] / l_scr[...]
    for h in range(nh):
        o_ref[0, :, h * KV_LORA:(h + 1) * KV_LORA] = o[h * t:(h + 1) * t].astype(o_ref.dtype)


def mla_prompt(q_cat, k_cat, tq):
    b, t, _ = q_cat.shape
    rows = MLA_HEADS * tq
    return pl.pallas_call(
        functools.partial(_mla_prompt_kernel, t=tq),
        grid=(b, t // tq),
        in_specs=[pl.BlockSpec((1, tq, MLA_HEADS * QCAT), lambda i, j: (i, j, 0)),
                  pl.BlockSpec((1, t, QCAT), lambda i, j: (i, 0, 0))],
        out_specs=pl.BlockSpec((1, tq, MLA_HEADS * KV_LORA), lambda i, j: (i, j, 0)),
        out_shape=jax.ShapeDtypeStruct((b, t, MLA_HEADS * KV_LORA), BF16),
        scratch_shapes=[pltpu.VMEM((rows, LANES), F32), pltpu.VMEM((rows, LANES), F32), pltpu.VMEM((rows, KV_LORA), F32)],
        compiler_params=_cp(("parallel", "arbitrary")),
        name="mla_prompt",
    )(q_cat, k_cat)


def _mla_sample_kernel(pt_ref, q_ref, knew_ref, ckv_hbm, krt_hbm, o_ref, ckv_buf, kr_buf, s_scr, sem,
                       *, layer, n_pages, n_seq):
    b = pl.program_id(0)
    slot = b % 2

    def start_page(seq, sl, i):
        page = pt_ref[seq * n_pages + i]
        pltpu.make_async_copy(ckv_hbm.at[layer, page], ckv_buf.at[sl, i], sem.at[sl, 0]).start()
        pltpu.make_async_copy(krt_hbm.at[layer, page], kr_buf.at[sl, i], sem.at[sl, 1]).start()

    @pl.when(b == 0)
    def _():
        def body(i, carry):
            start_page(0, 0, i)
            return carry
        lax.fori_loop(0, n_pages, body, 0)

    pltpu.make_async_copy(ckv_hbm.at[layer, pl.ds(0, n_pages)], ckv_buf.at[slot], sem.at[slot, 0]).wait()
    pltpu.make_async_copy(krt_hbm.at[layer, pl.ds(0, n_pages)], kr_buf.at[slot], sem.at[slot, 1]).wait()

    q = q_ref[0]
    ql = q[:, :KV_LORA]
    qr = q[:, KV_LORA:KV_LORA + QK_ROPE_DIM]

    def score_pages(prefetch):
        for pi in range(n_pages):
            if prefetch:
                start_page(b + 1, 1 - slot, pi)
            s_scr[pi] = _dot_nt(ql, _bf(ckv_buf[slot, pi])) + _dot(qr, _bf(kr_buf[slot, pi]))

    @pl.when(b + 1 < n_seq)
    def _():
        score_pages(True)

    @pl.when(b + 1 == n_seq)
    def _():
        score_pages(False)

    k_new = knew_ref[0].astype(F32)
    s_new = jnp.sum(q.astype(F32) * k_new, axis=1, keepdims=True)
    s = s_scr[...]
    m = jnp.maximum(jnp.max(jnp.max(s, axis=0), axis=1, keepdims=True), s_new)
    p = jnp.exp(s - m)
    p_new = jnp.exp(s_new - m)
    l = jnp.sum(jnp.sum(p, axis=0), axis=1, keepdims=True) + p_new
    s_scr[...] = p

    acc = p_new * k_new[:, :KV_LORA]
    for pi in range(n_pages):
        acc = acc + _dot(_bf(s_scr[pi]), _bf(ckv_buf[slot, pi]))
    o_ref[0] = (acc / l).astype(o_ref.dtype)


def mla_sample(page_table_flat, q, k_new, cache_ckv, cache_krope_t, layer, n_pages):
    n_seq = q.shape[0]
    grid_spec = pltpu.PrefetchScalarGridSpec(
        num_scalar_prefetch=1,
        grid=(n_seq,),
        in_specs=[pl.BlockSpec((1, MLA_HEADS, QCAT), lambda bi, pt: (bi, 0, 0)),
                  pl.BlockSpec((1, 1, QCAT), lambda bi, pt: (bi, 0, 0)),
                  pl.BlockSpec(memory_space=pl.ANY),
                  pl.BlockSpec(memory_space=pl.ANY)],
        out_specs=pl.BlockSpec((1, MLA_HEADS, KV_LORA), lambda bi, pt: (bi, 0, 0)),
        scratch_shapes=[pltpu.VMEM((2, n_pages, PAGE_SIZE, KV_LORA), F32),
                        pltpu.VMEM((2, n_pages, QK_ROPE_DIM, PAGE_SIZE), F32),
                        pltpu.VMEM((n_pages, MLA_HEADS, PAGE_SIZE), F32),
                        pltpu.SemaphoreType.DMA((2, 2))],
    )
    return pl.pallas_call(
        functools.partial(_mla_sample_kernel, layer=layer, n_pages=n_pages, n_seq=n_seq),
        grid_spec=grid_spec,
        out_shape=jax.ShapeDtypeStruct((n_seq, MLA_HEADS, KV_LORA), BF16),
        compiler_params=_cp(("arbitrary",)),
        name="mla_sample",
    )(page_table_flat, q, k_new, cache_ckv, cache_krope_t)


def _cross_sample_kernel(q_ref, mk_ref, mv_ref, o_ref):
    scale = MEM_HEAD_DIM ** -0.5
    q = q_ref[0].astype(F32)
    q4 = jnp.concatenate([q[:, h * MEM_HEAD_DIM:(h + 1) * MEM_HEAD_DIM] for h in range(MEM_HEADS)], axis=0)
    s = jnp.sum(mk_ref[0, 0] * q4[None], axis=2, keepdims=True) * scale
    m = jnp.max(s, axis=0, keepdims=True)
    p = jnp.exp(s - m)
    l = jnp.sum(p, axis=0)
    o = jnp.sum(p * mv_ref[0, 0], axis=0) / l
    for h in range(MEM_HEADS):
        o_ref[0, :, h * MEM_HEAD_DIM:(h + 1) * MEM_HEAD_DIM] = o[h:h + 1].astype(o_ref.dtype)


def cross_sample(q, mem_k, mem_v, layer):
    b, _, d = q.shape
    blk = pl.BlockSpec((1, 1, N_MEM, MEM_HEADS, MEM_HEAD_DIM), lambda i: (layer, i, 0, 0, 0))
    return pl.pallas_call(
        _cross_sample_kernel,
        grid=(b,),
        in_specs=[pl.BlockSpec((1, 1, d), lambda i: (i, 0, 0)), blk, blk],
        out_specs=pl.BlockSpec((1, 1, d), lambda i: (i, 0, 0)),
        out_shape=jax.ShapeDtypeStruct((b, 1, d), BF16),
        compiler_params=_cp(("parallel",)),
        name="cross_sample",
    )(q, mem_k, mem_v)


def _rot_cols(w):
    half = QK_ROPE_DIM // 2
    return jnp.concatenate([-w[..., half:], w[..., :half]], axis=-1)


def _pad_lanes(w):
    return jnp.pad(w, [(0, 0)] * (w.ndim - 1) + [(0, LANES - w.shape[-1])])


def _layer_weights(l, w_in, shift_mu, rwkv_w0, rwkv_w_up, rwkv_a0, rwkv_a_up, rwkv_g_up, rwkv_k_k, rwkv_k_a,
                   rwkv_r_k, rwkv_gn_w, rwkv_gn_b, mla_q_norm, mla_w_uq, mla_kv_norm, mla_w_uk, mla_w_uv):
    w3 = RWKV_WIDTH
    o_kr = RWKV_PROJ + Q_LORA + KV_LORA
    w_kr = w_in[l][:, o_kr:]
    w_in_ext = _bf(jnp.concatenate([w_in[l][:, :o_kr], _pad_lanes(w_kr), _pad_lanes(_rot_cols(w_kr))], axis=1))
    w_lora = jnp.zeros((LORA_SLAB, 3 * w3), F32)
    w_lora = w_lora.at[:DECAY_LORA, :w3].set(rwkv_w_up[l])
    w_lora = w_lora.at[DECAY_LORA:DECAY_LORA + ICLR_LORA, w3:2 * w3].set(rwkv_a_up[l])
    w_lora = w_lora.at[DECAY_LORA + ICLR_LORA:, 2 * w3:].set(rwkv_g_up[l])
    uq = mla_w_uq[l].reshape(Q_LORA, MLA_HEADS, QK_NOPE_DIM + QK_ROPE_DIM)
    uq_nope = uq[:, :, :QK_NOPE_DIM].reshape(Q_LORA, MLA_HEADS * QK_NOPE_DIM)
    uq_rope = uq[:, :, QK_NOPE_DIM:]
    w_q = _bf(jnp.concatenate([uq_nope,
                               _pad_lanes(uq_rope).reshape(Q_LORA, MLA_HEADS * LANES),
                               _pad_lanes(_rot_cols(uq_rope)).reshape(Q_LORA, MLA_HEADS * LANES)], axis=1))
    eye_h = jnp.eye(MLA_HEADS, dtype=F32)
    wuk_bd = _bf(jnp.einsum('chd,hg->hdgc', mla_w_uk[l], eye_h).reshape(MLA_HEADS * QK_NOPE_DIM, MLA_HEADS * KV_LORA))
    wuv_bd = _bf(jnp.einsum('chd,hg->hcgd', mla_w_uv[l], eye_h).reshape(MLA_HEADS * KV_LORA, MLA_HEADS * V_HEAD_DIM))
    row = lambda a: a.reshape(1, -1)
    return dict(
        w_in_ext=w_in_ext, mu=row(shift_mu[l]), w0=row(rwkv_w0[l]), a0=row(rwkv_a0[l]), w_lora=_bf(w_lora),
        k_k=row(rwkv_k_k[l]), k_a=row(rwkv_k_a[l]), rk=row(rwkv_r_k[l]), gnw=row(rwkv_gn_w[l]), gnb=row(rwkv_gn_b[l]),
        q_norm=row(mla_q_norm[l]), kv_norm=row(mla_kv_norm[l]), w_q=w_q, wuk_bd=wuk_bd, wuv_bd=wuv_bd)


def _rope_tables(pos):
    half = QK_ROPE_DIM // 2
    inv_freq = ROPE_THETA ** (-jnp.arange(half, dtype=F32) / half)
    ang = pos.astype(F32)[:, None] * inv_freq[None, :]
    cos, sin = jnp.cos(ang), jnp.sin(ang)
    return (_pad_lanes(jnp.concatenate([cos, cos], axis=1)), _pad_lanes(jnp.concatenate([sin, sin], axis=1)))


def kernel(x_prompt, x_sample, cache_ckv, cache_krope, state_wkv, state_shift, cache_mem_k, cache_mem_v, page_table, mem_prompt, norm_ffn1, ffn1_w_gate, ffn1_w_up, ffn1_w_down, norm_mix, w_in, shift_mu, rwkv_w0, rwkv_w_up, rwkv_a0, rwkv_a_up, rwkv_g_up, rwkv_k_k, rwkv_k_a, rwkv_r_k, rwkv_gn_w, rwkv_gn_b, mla_q_norm, mla_w_uq, mla_kv_norm, mla_w_uk, mla_w_uv, w_out, norm_cross, norm_mem, mem_w_q, mem_w_k, mem_w_v, mem_w_o, norm_ffn2, ffn2_w_gate, ffn2_w_up, ffn2_w_down, norm_final):
    bp, tp, d = x_prompt.shape
    bs, ts, _ = x_sample.shape
    assert ts == 1
    n_pages = page_table.shape[1]
    past_len = n_pages * PAGE_SIZE
    hh, nn = RWKV_HEADS, RWKV_HEAD_DIM
    row = lambda a: a.reshape(1, -1)

    head_of = jnp.arange(RWKV_WIDTH) // nn
    ones_bd = (head_of[:, None] == head_of[None, :]).astype(BF16)
    cos_p, sin_p = _rope_tables(jnp.arange(tp))
    cos_s, sin_s = _rope_tables(jnp.full((bs,), past_len, jnp.int32))
    pt_flat = page_table.reshape(-1)
    krope_t = jnp.swapaxes(cache_krope, 2, 3)

    lw = [_layer_weights(l, w_in, shift_mu, rwkv_w0, rwkv_w_up, rwkv_a0, rwkv_a_up, rwkv_g_up, rwkv_k_k, rwkv_k_a,
                         rwkv_r_k, rwkv_gn_w, rwkv_gn_b, mla_q_norm, mla_w_uq, mla_kv_norm, mla_w_uk, mla_w_uv)
          for l in range(DEPTH)]

    tm_p, tm_s = 512, bs

    def run_ffn(x2, l, which, tm, final):
        if which == 1:
            g, wg, wu, wd = norm_ffn1[l], ffn1_w_gate, ffn1_w_up, ffn1_w_down
        else:
            g, wg, wu, wd = norm_ffn2[l], ffn2_w_gate, ffn2_w_up, ffn2_w_down
        return ffn(x2, row(g), wg, wu, wd, l, row(norm_final), final, tm)

    mem_k_all, mem_v_all = memory_kv(mem_prompt.reshape(bp * N_MEM, d), norm_mem.reshape(DEPTH, 1, d), mem_w_k, mem_w_v, 512)
    mem_k_p = mem_k_all.reshape(DEPTH, bp, N_MEM, d)
    mem_v_p = mem_v_all.reshape(DEPTH, bp, N_MEM, d)

    def trunk(x, sample):
        b, t, _ = x.shape
        m = b * t
        tm = tm_s if sample else tm_p
        x2 = x.reshape(m, d)
        ckvs, krs, wkvs, shifts = [], [], [], []
        for l in range(DEPTH):
            p = lw[l]
            x2 = run_ffn(x2, l, 1, tm, False)
            if sample:
                outs = prep(x2.reshape(1, m, d), row(norm_mix[l]), p['w_in_ext'],
                            state_shift[l].reshape(1, m, RWKV_PROJ), True,
                            p['mu'], p['w0'], p['a0'], p['w_lora'], p['k_k'], p['k_a'], ones_bd, p['q_norm'],
                            p['kv_norm'], p['w_q'], p['wuk_bd'], cos_s, sin_s, m)
            else:
                outs = prep(x2.reshape(b, t, d), row(norm_mix[l]), p['w_in_ext'],
                            jnp.zeros((b, 1, RWKV_PROJ), F32), False,
                            p['mu'], p['w0'], p['a0'], p['w_lora'], p['k_k'], p['k_a'], ones_bd, p['q_norm'],
                            p['kv_norm'], p['w_q'], p['wuk_bd'], cos_p, sin_p, 256)
            shift_last = outs[11].reshape(b, RWKV_PROJ)
            r, k, v, kk, bv, lwd, g, ckv, kr, q_cat, k_cat = outs[:11]
            if sample:
                to_heads = lambda a: a.reshape(m, hh, nn)
                o_rwkv, wkv = rwkv_step(to_heads(r), to_heads(k), to_heads(v), to_heads(kk), to_heads(bv),
                                        to_heads(lwd), to_heads(g), p['rk'].reshape(hh, nn), p['gnw'].reshape(hh, nn),
                                        p['gnb'].reshape(hh, nn), state_wkv, l, 8)
                o_rwkv = _bf(o_rwkv.reshape(m, RWKV_WIDTH))
                o_lat = mla_sample(pt_flat, q_cat.reshape(m, MLA_HEADS, QCAT), k_cat.reshape(m, 1, QCAT),
                                   cache_ckv, krope_t, l, n_pages).reshape(m, MLA_HEADS * KV_LORA)
                x2 = mix_out(o_rwkv, o_lat, p['wuv_bd'], _bf(w_out[l]), x2, tm)
                q = norm_matmul(x2, row(norm_cross[l]), _bf(mem_w_q[l]), F32, tm)
                att = cross_sample(q.reshape(b, t, d), cache_mem_k, cache_mem_v, l)
                x2 = matmul_res(att.reshape(m, d), _bf(mem_w_o[l]), x2, tm)
            else:
                o_rwkv, s_bd = rwkv_chunked(r, k, v, kk, bv, lwd, g, p['rk'], p['gnw'], p['gnb'], CHUNK, CHUNK_SEQS)
                wkv = jnp.stack([s_bd[:, :, :nn, :nn], s_bd[:, :, nn:, nn:]], axis=2).reshape(b, hh, nn, nn)
                o_lat = mla_prompt(q_cat, k_cat, 256)
                x2 = post_mix(o_rwkv, o_lat, x2.reshape(b, t, d), p['wuv_bd'], _bf(w_out[l]), row(norm_cross[l]),
                              _bf(mem_w_q[l]), mem_k_p, mem_v_p, l, _bf(mem_w_o[l]), 512).reshape(m, d)
            x2 = run_ffn(x2, l, 2, tm, l == DEPTH - 1)
            ckvs.append(ckv.reshape(b, t, KV_LORA))
            krs.append(kr.reshape(b, t, QK_ROPE_DIM))
            wkvs.append(wkv)
            shifts.append(shift_last)
        return x2.reshape(b, t, d), jnp.stack(ckvs), jnp.stack(krs), jnp.stack(wkvs), jnp.stack(shifts)

    y_p, ckv_p, kr_p, wkv_p, shift_p = trunk(x_prompt, False)
    y_s, ckv_s, kr_s, wkv_s, shift_s = trunk(x_sample, True)
    mem_shape = (DEPTH, bp, N_MEM, MEM_HEADS, MEM_HEAD_DIM)
    return (y_p, y_s, ckv_p, kr_p, wkv_p, shift_p,
            mem_k_all.reshape(mem_shape), mem_v_all.reshape(mem_shape),
            ckv_s, kr_s, wkv_s, shift_s)
```

```python
import functools

import jax
import jax.numpy as jnp
from jax import lax
from jax.experimental import pallas as pl
from jax.experimental.pallas import tpu as pltpu

F32 = jnp.float32
BF16 = jnp.bfloat16

D_MODEL = 1024
DEPTH = 2
PAGE_SIZE = 128
RWKV_HEADS = 8
RWKV_HEAD_DIM = 64
RWKV_WIDTH = 512
DECAY_LORA = 64
ICLR_LORA = 64
GATE_LORA = 128
GN_EPS = 64e-5
MLA_HEADS = 8
QK_NOPE_DIM = 64
QK_ROPE_DIM = 32
V_HEAD_DIM = 64
Q_LORA = 256
KV_LORA = 128
ROPE_THETA = 10000.0
MLA_SCALE = (QK_NOPE_DIM + QK_ROPE_DIM) ** -0.5
RWKV_PROJ = 3 * RWKV_WIDTH + DECAY_LORA + ICLR_LORA + GATE_LORA
N_MEM = 256
MEM_HEADS = 4
MEM_HEAD_DIM = D_MODEL // MEM_HEADS
D_FF = 2816
RMS_EPS = 1e-6

LANES = 128
IN_PROJ_EXT = RWKV_PROJ + Q_LORA + KV_LORA + 2 * LANES
LORA_SLAB = DECAY_LORA + ICLR_LORA + GATE_LORA
QCAT = 2 * LANES
CHUNK = 64
CHUNK_SEQS = 4
FFN_CHUNK = 256
NEG = -1e30
VMEM_LIMIT = 56 * 1024 * 1024


def _cp(sem):
    return pltpu.CompilerParams(dimension_semantics=sem, vmem_limit_bytes=VMEM_LIMIT)


def _dot(a, b):
    return jnp.dot(a, b, preferred_element_type=F32)


def _dot_nt(a, b):
    return lax.dot_general(a, b, (((1,), (1,)), ((), ())), preferred_element_type=F32)


def _bf(x):
    return x.astype(BF16)


def _split3(x):
    hi = _bf(x)
    r1 = x - hi.astype(F32)
    mid = _bf(r1)
    return hi, mid, _bf(r1 - mid.astype(F32))


def _dot_split2(x, w):
    hi = _bf(x)
    return _dot(hi, w) + _dot(_bf(x - hi.astype(F32)), w)


def _rms(x, g):
    return x * lax.rsqrt(jnp.mean(x * x, axis=-1, keepdims=True) + RMS_EPS) * g


def _sigmoid(x):
    return 1.0 / (1.0 + jnp.exp(-x))


def _norm_matmul_kernel(x_ref, g_ref, w_ref, o_ref):
    h = _bf(_rms(x_ref[...], g_ref[...]))
    o_ref[...] = _dot(h, w_ref[...]).astype(o_ref.dtype)


def norm_matmul(x, g, w, out_dtype, tm):
    m, d = x.shape
    n = w.shape[1]
    return pl.pallas_call(
        _norm_matmul_kernel,
        grid=(m // tm,),
        in_specs=[pl.BlockSpec((tm, d), lambda i: (i, 0)),
                  pl.BlockSpec((1, d), lambda i: (0, 0)),
                  pl.BlockSpec((d, n), lambda i: (0, 0))],
        out_specs=pl.BlockSpec((tm, n), lambda i: (i, 0)),
        out_shape=jax.ShapeDtypeStruct((m, n), out_dtype),
        compiler_params=_cp(("parallel",)),
        name="norm_matmul",
    )(x, g, w)


def _memory_kv_kernel(m_ref, g_ref, wk_ref, wv_ref, k_o, v_o):
    h = _bf(_rms(m_ref[...], g_ref[...]))
    k_o[...] = _dot(h, _bf(wk_ref[...]))
    v_o[...] = _dot(h, _bf(wv_ref[...]))


def memory_kv(mem, g, w_k, w_v, tm):
    rows, d = mem.shape
    depth = g.shape[0]
    per_layer = lambda shape: pl.BlockSpec((None,) + shape, lambda l, i: (l, 0, 0))
    out = pl.BlockSpec((None, tm, d), lambda l, i: (l, i, 0))
    return pl.pallas_call(
        _memory_kv_kernel,
        grid=(depth, rows // tm),
        in_specs=[pl.BlockSpec((tm, d), lambda l, i: (i, 0)), per_layer((1, d)), per_layer((d, d)), per_layer((d, d))],
        out_specs=[out, out],
        out_shape=[jax.ShapeDtypeStruct((depth, rows, d), F32)] * 2,
        compiler_params=_cp(("parallel", "parallel")),
        name="memory_kv",
    )(mem, g, w_k, w_v)


def _ffn_kernel(x_ref, g_ref, wg_ref, wu_ref, wd_ref, gf_ref, o_ref, act_scr, *, final_norm):
    x = x_ref[...]
    h = _bf(_rms(x, g_ref[...]))
    dff = wg_ref.shape[1]
    for c0 in range(0, dff, FFN_CHUNK):
        cs = slice(c0, min(c0 + FFN_CHUNK, dff))
        gate = _dot(h, _bf(wg_ref[:, cs]))
        up = _dot(h, _bf(wu_ref[:, cs]))
        act_scr[:, cs] = _bf(gate * _sigmoid(gate) * up)
    y = x + 0.5 * _dot(act_scr[...], _bf(wd_ref[...]))
    if final_norm:
        y = _rms(y, gf_ref[...])
    o_ref[...] = y


def _resident(shape):
    return pl.BlockSpec(shape, lambda *_: (0,) * len(shape), pipeline_mode=pl.Buffered(1))


def _layer_resident(arr, layer):
    return pl.BlockSpec((None,) + arr.shape[1:], lambda *_: (layer,) + (0,) * (arr.ndim - 1),
                        pipeline_mode=pl.Buffered(1))


def ffn(x, g, wg, wu, wd, layer, gf, final_norm, tm):
    m, d = x.shape
    dff = wg.shape[2]
    return pl.pallas_call(
        functools.partial(_ffn_kernel, final_norm=final_norm),
        grid=(m // tm,),
        in_specs=[pl.BlockSpec((tm, d), lambda i: (i, 0)), _resident((1, d)), _layer_resident(wg, layer),
                  _layer_resident(wu, layer), _layer_resident(wd, layer), _resident((1, d))],
        out_specs=pl.BlockSpec((tm, d), lambda i: (i, 0)),
        out_shape=jax.ShapeDtypeStruct((m, d), F32),
        scratch_shapes=[pltpu.VMEM((tm, dff), BF16)],
        compiler_params=_cp(("parallel",)),
        name="ffn",
    )(x, g, wg, wu, wd, gf)


def _matmul_res_kernel(a_ref, w_ref, x_ref, o_ref):
    o_ref[...] = x_ref[...] + _dot(a_ref[...], w_ref[...])


def matmul_res(a, w, x, tm):
    m, k = a.shape
    n = w.shape[1]
    return pl.pallas_call(
        _matmul_res_kernel,
        grid=(m // tm,),
        in_specs=[pl.BlockSpec((tm, k), lambda i: (i, 0)),
                  pl.BlockSpec((k, n), lambda i: (0, 0)),
                  pl.BlockSpec((tm, n), lambda i: (i, 0))],
        out_specs=pl.BlockSpec((tm, n), lambda i: (i, 0)),
        out_shape=jax.ShapeDtypeStruct((m, n), F32),
        compiler_params=_cp(("parallel",)),
        name="matmul_res",
    )(a, w, x)


def _mix_out_kernel(orw_ref, olat_ref, wuv_ref, wout_ref, x_ref, o_ref):
    o_mla = _bf(_dot(olat_ref[...], wuv_ref[...]))
    y = _dot(orw_ref[...], wout_ref[:RWKV_WIDTH, :]) + _dot(o_mla, wout_ref[RWKV_WIDTH:, :])
    o_ref[...] = x_ref[...] + y


def mix_out(o_rwkv, o_lat, wuv_bd, w_out, x, tm):
    m, d = x.shape
    return pl.pallas_call(
        _mix_out_kernel,
        grid=(m // tm,),
        in_specs=[pl.BlockSpec((tm, RWKV_WIDTH), lambda i: (i, 0)),
                  pl.BlockSpec((tm, MLA_HEADS * KV_LORA), lambda i: (i, 0)),
                  pl.BlockSpec(wuv_bd.shape, lambda i: (0, 0)),
                  pl.BlockSpec(w_out.shape, lambda i: (0, 0)),
                  pl.BlockSpec((tm, d), lambda i: (i, 0))],
        out_specs=pl.BlockSpec((tm, d), lambda i: (i, 0)),
        out_shape=jax.ShapeDtypeStruct((m, d), F32),
        compiler_params=_cp(("parallel",)),
        name="mix_out",
    )(o_rwkv, o_lat, wuv_bd, w_out, x)


def _post_mix_kernel(orw_ref, olat_ref, x_ref, wuv_ref, wout_ref, gc_ref, wq_ref, mk_ref, mv_ref, wo_ref, o_ref):
    o_mla = _bf(_dot(olat_ref[0], wuv_ref[...]))
    x1 = x_ref[0] + _dot(orw_ref[0], wout_ref[:RWKV_WIDTH, :]) + _dot(o_mla, wout_ref[RWKV_WIDTH:, :])
    q = _bf(_dot(_bf(_rms(x1, gc_ref[...])), wq_ref[...]))
    scale = MEM_HEAD_DIM ** -0.5
    att = []
    for h in range(MEM_HEADS):
        hs = slice(h * MEM_HEAD_DIM, (h + 1) * MEM_HEAD_DIM)
        s = _dot_nt(q[:, hs], _bf(mk_ref[0, :, hs])) * scale
        p = jnp.exp(s - jnp.max(s, axis=1, keepdims=True))
        l = jnp.sum(p, axis=1, keepdims=True)
        att.append(_bf(_dot(_bf(p), _bf(mv_ref[0, :, hs])) / l))
    o_ref[0] = x1 + _dot(jnp.concatenate(att, axis=1), wo_ref[...])


def post_mix(o_rwkv, o_lat, x, wuv_bd, w_out, g_cross, w_q, mem_k, mem_v, layer, w_o, tq):
    b, t, d = x.shape
    seq = lambda w: pl.BlockSpec((1, tq, w), lambda i, j: (i, j, 0))
    mem = pl.BlockSpec((None, 1, N_MEM, d), lambda i, j: (layer, i, 0, 0))
    return pl.pallas_call(
        _post_mix_kernel,
        grid=(b, t // tq),
        in_specs=[seq(RWKV_WIDTH), seq(MLA_HEADS * KV_LORA), seq(d), _resident(wuv_bd.shape), _resident(w_out.shape),
                  _resident(g_cross.shape), _resident(w_q.shape), mem, mem, _resident(w_o.shape)],
        out_specs=seq(d),
        out_shape=jax.ShapeDtypeStruct((b, t, d), F32),
        compiler_params=_cp(("parallel", "arbitrary")),
        name="post_mix",
    )(o_rwkv, o_lat, x, wuv_bd, w_out, g_cross, w_q, mem_k, mem_v, w_o)


def _prep_kernel(x_ref, gmix_ref, win_ref, prev_ref, mu_ref, w0_ref, a0_ref, wl_ref, kk_ref, ka_ref, ones_ref,
                 qn_ref, kvn_ref, wq_ref, wuk_ref, cos_ref, sin_ref,
                 r_o, k_o, v_o, kkn_o, b_o, lw_o, g_o, ckv_o, kr_o, qcat_o, kcat_o, raw_o,
                 proj_ref, carry, *, tm, explicit_prev):
    proj_ref[...] = _dot(_bf(_rms(x_ref[0], gmix_ref[...])), win_ref[...])
    p = proj_ref[:, :RWKV_PROJ]
    if explicit_prev:
        prev = prev_ref[0]
        raw_o[0] = p
    else:
        @pl.when(pl.program_id(1) == 0)
        def _():
            carry[...] = prev_ref[0]

        row = lax.broadcasted_iota(jnp.int32, (tm, 1), 0)
        prev = jnp.where(row == 0, carry[...], pltpu.roll(p, 1, axis=0))
        carry[...] = p[tm - 1:tm, :]
        raw_o[0] = p[tm - 1:tm, :]
    ps = p + (prev - p) * mu_ref[...]

    w3 = RWKV_WIDTH
    r = ps[:, :w3]
    k = ps[:, w3:2 * w3]
    v = ps[:, 2 * w3:3 * w3]
    slab = ps[:, 3 * w3:]
    lane = lax.broadcasted_iota(jnp.int32, slab.shape, 1)
    act = jnp.where(lane < DECAY_LORA, jnp.tanh(slab),
                    jnp.where(lane < DECAY_LORA + ICLR_LORA, slab, _sigmoid(slab)))
    lo = _dot_split2(act, wl_ref[...])
    z = -(w0_ref[...] + lo[:, :w3])
    softplus = jnp.maximum(z, 0.0) + jnp.log(1.0 + jnp.exp(-jnp.abs(z)))
    lw = -jnp.exp(-softplus - 0.5)
    a = _sigmoid(a0_ref[...] + lo[:, w3:2 * w3])
    g = lo[:, 2 * w3:]
    kkr = k * kk_ref[...]
    ssq = _dot_split2(kkr * kkr, ones_ref[...])
    kkn = kkr / jnp.maximum(jnp.sqrt(ssq), 1e-12)
    r_o[0] = r
    k_o[0] = k * (1.0 + (a - 1.0) * ka_ref[...])
    v_o[0] = v
    kkn_o[0] = kkn
    b_o[0] = kkn * a
    lw_o[0] = lw
    g_o[0] = g

    o1 = RWKV_PROJ
    cq = _bf(_rms(proj_ref[:, o1:o1 + Q_LORA], qn_ref[...]))
    o2 = o1 + Q_LORA
    ckv = _rms(proj_ref[:, o2:o2 + KV_LORA], kvn_ref[...])
    o3 = o2 + KV_LORA
    cos = cos_ref[...]
    sin = sin_ref[...]
    kr = proj_ref[:, o3:o3 + LANES] * cos + proj_ref[:, o3 + LANES:o3 + 2 * LANES] * sin
    ckv_o[0] = ckv
    kr_o[0] = kr[:, :QK_ROPE_DIM]
    kcat_o[0] = _bf(jnp.concatenate([ckv, kr], axis=1))

    qq = _dot(cq, wq_ref[...])
    nn = MLA_HEADS * QK_NOPE_DIM
    nr = MLA_HEADS * LANES
    q_lat = _dot(_bf(qq[:, :nn]), wuk_ref[...]) * MLA_SCALE
    cos8 = jnp.concatenate([cos] * MLA_HEADS, axis=1)
    sin8 = jnp.concatenate([sin] * MLA_HEADS, axis=1)
    q_rp = (qq[:, nn:nn + nr] * cos8 + qq[:, nn + nr:] * sin8) * MLA_SCALE
    pieces = []
    for h in range(MLA_HEADS):
        pieces.append(q_lat[:, h * LANES:(h + 1) * LANES])
        pieces.append(q_rp[:, h * LANES:(h + 1) * LANES])
    qcat_o[0] = _bf(jnp.concatenate(pieces, axis=1))


def prep(x, g_mix, w_in_ext, prev, explicit_prev, mu, w0, a0, w_lora, k_k, k_a, ones_bd, q_norm, kv_norm, w_q, wuk_bd,
         cos_t, sin_t, tm):
    b, t, d = x.shape
    nt = t // tm
    wide = lambda w: pl.BlockSpec((1, tm, w), lambda i, j: (i, j, 0))
    full = lambda arr: _resident(arr.shape)
    if explicit_prev:
        prev_spec = wide(RWKV_PROJ)
        raw_spec, raw_rows = wide(RWKV_PROJ), t
    else:
        prev_spec = pl.BlockSpec((1, 1, RWKV_PROJ), lambda i, j: (i, 0, 0))
        raw_spec, raw_rows = pl.BlockSpec((1, 1, RWKV_PROJ), lambda i, j: (i, 0, 0)), 1
    sd = lambda w, dt: jax.ShapeDtypeStruct((b, t, w), dt)
    w3 = RWKV_WIDTH
    out_shape = [sd(w3, F32)] * 7 + [sd(KV_LORA, F32), sd(QK_ROPE_DIM, F32),
                                      sd(MLA_HEADS * QCAT, BF16), sd(QCAT, BF16),
                                      jax.ShapeDtypeStruct((b, raw_rows, RWKV_PROJ), F32)]
    out_specs = [wide(w3)] * 7 + [wide(KV_LORA), wide(QK_ROPE_DIM), wide(MLA_HEADS * QCAT), wide(QCAT), raw_spec]
    return pl.pallas_call(
        functools.partial(_prep_kernel, tm=tm, explicit_prev=explicit_prev),
        grid=(b, nt),
        in_specs=[wide(d), full(g_mix), full(w_in_ext), prev_spec, full(mu), full(w0), full(a0), full(w_lora),
                  full(k_k), full(k_a), full(ones_bd), full(q_norm), full(kv_norm), full(w_q), full(wuk_bd),
                  pl.BlockSpec((tm, LANES), lambda i, j: (j, 0)),
                  pl.BlockSpec((tm, LANES), lambda i, j: (j, 0))],
        out_specs=out_specs,
        out_shape=out_shape,
        scratch_shapes=[pltpu.VMEM((tm, IN_PROJ_EXT), F32), pltpu.VMEM((1, RWKV_PROJ), F32)],
        compiler_params=_cp(("arbitrary", "arbitrary")),
        name="prep",
    )(x, g_mix, w_in_ext, prev, mu, w0, a0, w_lora, k_k, k_a, ones_bd, q_norm, kv_norm, w_q, wuk_bd, cos_t, sin_t)


def _rwkv_chunk_kernel(r_ref, k_ref, v_ref, kk_ref, b_ref, lw_ref, g_ref, rk_ref, gnw_ref, gnb_ref,
                       o_ref, s_out_ref, s_scr, *, c, n_chunks, nb):
    @pl.when(pl.program_id(1) == 0)
    def _():
        s_scr[...] = jnp.zeros_like(s_scr)

    n = RWKV_HEAD_DIM
    npair = RWKV_HEADS // 2
    row = lax.broadcasted_iota(jnp.int32, (c, c), 0)
    col = lax.broadcasted_iota(jnp.int32, (c, c), 1)
    incl = row >= col
    strict = row > col
    tri = _bf(jnp.where(incl, 1.0, 0.0))
    eye = jnp.where(row == col, 1.0, 0.0).astype(F32)
    lane = lax.broadcasted_iota(jnp.int32, (c, LANES), 1)
    is_e = lane < n
    r2 = lax.broadcasted_iota(jnp.int32, (LANES, LANES), 0)
    c2 = lax.broadcasted_iota(jnp.int32, (LANES, LANES), 1)
    same_head = (r2 < n) == (c2 < n)
    ones_bd = _bf(jnp.where(same_head, 1.0, 0.0))
    n_double = c.bit_length() - 2
    pairs = range(nb * npair)
    heads = [(j, x) for j in pairs for x in range(2)]
    sl = [slice((j % npair) * LANES, (j % npair + 1) * LANES) for j in pairs]
    bi = [j // npair for j in pairs]

    r = [r_ref[bi[j], :, sl[j]] for j in pairs]
    k = [k_ref[bi[j], :, sl[j]] for j in pairs]
    v = [v_ref[bi[j], :, sl[j]] for j in pairs]
    kk = [kk_ref[bi[j], :, sl[j]] for j in pairs]
    bv = [b_ref[bi[j], :, sl[j]] for j in pairs]
    lw = [lw_ref[bi[j], :, sl[j]] for j in pairs]
    s0 = [s_scr[bi[j], j % npair] for j in pairs]

    cum = [sum(_dot(tri, part) for part in _split3(lw[j])) for j in pairs]
    e_pos = [jnp.exp(cum[j]) for j in pairs]
    e_neg = [jnp.exp(-cum[j]) for j in pairs]
    a_t = [-kk[j] * jnp.exp(cum[j] - lw[j]) for j in pairs]
    r_t = [r[j] * e_pos[j] for j in pairs]
    b_t = [bv[j] * e_neg[j] for j in pairs]
    k_t = [k[j] * e_neg[j] for j in pairs]
    g_c = [e_pos[j][c - 1:c, :] for j in pairs]

    v_b = [_bf(v[j]) for j in pairs]
    s0_b = [_bf(s0[j]) for j in pairs]
    zero = jnp.zeros((c, LANES), F32)
    xb, xk = [], []
    for j in pairs:
        lhs = _bf(jnp.concatenate([jnp.where(is_e, a_t[j], zero), jnp.where(is_e, r_t[j], zero),
                                   jnp.where(is_e, zero, a_t[j]), jnp.where(is_e, zero, r_t[j])], axis=0))
        xb.append(_dot_nt(lhs, _bf(b_t[j])))
        xk.append(_dot_nt(lhs, _bf(k_t[j])))
    rs0 = [_dot_nt(_bf(r_t[j]), s0_b[j]) for j in pairs]

    nmat, arb, aakv, o2 = {}, {}, {}, {}
    for (j, x) in heads:
        base = 2 * c * x
        nmat[j, x] = jnp.where(strict, xb[j][base:base + c], 0.0)
        arb[j, x] = _bf(jnp.where(incl, xb[j][base + c:base + 2 * c], 0.0))
        aakv[j, x] = _dot(_bf(jnp.where(strict, xk[j][base:base + c], 0.0)), v_b[j])
        o2[j, x] = _dot(_bf(jnp.where(incl, xk[j][base + c:base + 2 * c], 0.0)), v_b[j])

    tinv = {h: eye + nmat[h] for h in heads}
    pw = dict(nmat)
    for _ in range(n_double):
        pw = {h: _dot(_bf(pw[h]), _bf(pw[h])) for h in heads}
        tinv = {h: tinv[h] + _dot(_bf(pw[h]), _bf(tinv[h])) for h in heads}

    w12 = {h: _dot(_bf(tinv[h]), _bf(jnp.concatenate([a_t[h[0]], aakv[h]], axis=1))) for h in heads}
    u = []
    for j in pairs:
        w1 = jnp.where(is_e, w12[j, 0][:, :LANES], w12[j, 1][:, :LANES])
        w2 = jnp.where(is_e, w12[j, 0][:, LANES:], w12[j, 1][:, LANES:])
        u.append(_dot_nt(_bf(w1), s0_b[j]) + w2)
    u_b = [_bf(u[j]) for j in pairs]
    o = [rs0[j] + jnp.where(is_e, _dot(arb[j, 0], u_b[j]) + o2[j, 0], _dot(arb[j, 1], u_b[j]) + o2[j, 1])
         for j in pairs]
    for j in pairs:
        uv = jnp.concatenate([u[j], v[j]], axis=0)
        bk = jnp.concatenate([b_t[j] * g_c[j], k_t[j] * g_c[j]], axis=0)
        s_scr[bi[j], j % npair] = s0[j] * g_c[j] + jnp.where(same_head, _dot(_bf(uv.T), _bf(bk)), 0.0)

    inv_n = 1.0 / n
    o_all = jnp.concatenate(o, axis=0)
    mean = _dot(_bf(o_all), ones_bd) * inv_n
    d = o_all - mean
    var = _dot(_bf(d * d), ones_bd) * inv_n
    bsum = _dot(_bf(jnp.concatenate([r[j] * k[j] * rk_ref[:, sl[j]] for j in pairs], axis=0)), ones_bd)
    on = d * lax.rsqrt(var + GN_EPS)
    for j in pairs:
        rows = slice(j * c, (j + 1) * c)
        out = (on[rows] * gnw_ref[:, sl[j]] + gnb_ref[:, sl[j]] + bsum[rows] * v[j]) * g_ref[bi[j], :, sl[j]]
        o_ref[bi[j], :, sl[j]] = out.astype(o_ref.dtype)

    @pl.when(pl.program_id(1) == n_chunks - 1)
    def _():
        s_out_ref[...] = s_scr[...]


def rwkv_chunked(r, k, v, kk, bv, lw, g, rk, gnw, gnb, c, nb):
    b, t, w = r.shape
    n_chunks = t // c
    seq = pl.BlockSpec((nb, c, w), lambda i, j: (i, j, 0))
    par = pl.BlockSpec((1, w), lambda i, j: (0, 0))
    npair = RWKV_HEADS // 2
    return pl.pallas_call(
        functools.partial(_rwkv_chunk_kernel, c=c, n_chunks=n_chunks, nb=nb),
        grid=(b // nb, n_chunks),
        in_specs=[seq] * 7 + [par] * 3,
        out_specs=[seq, pl.BlockSpec((nb, npair, LANES, LANES), lambda i, j: (i, 0, 0, 0))],
        out_shape=[jax.ShapeDtypeStruct((b, t, w), BF16),
                   jax.ShapeDtypeStruct((b, npair, LANES, LANES), F32)],
        scratch_shapes=[pltpu.VMEM((nb, npair, LANES, LANES), F32)],
        compiler_params=_cp(("arbitrary", "arbitrary")),
        name="rwkv_chunked",
    )(r, k, v, kk, bv, lw, g, rk, gnw, gnb)


def _rwkv_step_kernel(r_ref, k_ref, v_ref, kk_ref, b_ref, lw_ref, g_ref, rk_ref, gnw_ref, gnb_ref, s_ref,
                      o_ref, s_out_ref, *, bb):
    n = RWKV_HEAD_DIM
    eye = lax.broadcasted_iota(jnp.int32, (n, n), 0) == lax.broadcasted_iota(jnp.int32, (n, n), 1)

    def body(i, carry):
        for h in range(RWKV_HEADS):
            hs = slice(h, h + 1)
            s = s_ref[i, h]
            r = r_ref[i, hs, :]
            k = k_ref[i, hs, :]
            v = v_ref[i, hs, :]
            kk = kk_ref[i, hs, :]
            bv = b_ref[i, hs, :]
            w = jnp.exp(lw_ref[i, hs, :])
            s_kk = jnp.sum(s * kk, axis=1, keepdims=True)
            v_col = jnp.sum(jnp.where(eye, v, 0.0), axis=1, keepdims=True)
            s_new = s * w - s_kk * bv + v_col * k
            o_col = jnp.sum(s_new * r, axis=1, keepdims=True)
            o = jnp.sum(jnp.where(eye, o_col, 0.0), axis=0, keepdims=True)
            mean = jnp.mean(o, axis=1, keepdims=True)
            d = o - mean
            var = jnp.mean(d * d, axis=1, keepdims=True)
            on = d * lax.rsqrt(var + GN_EPS) * gnw_ref[hs, :] + gnb_ref[hs, :]
            bonus = jnp.sum(r * k * rk_ref[hs, :], axis=1, keepdims=True) * v
            o_ref[i, hs, :] = ((on + bonus) * g_ref[i, hs, :]).astype(o_ref.dtype)
            s_out_ref[i, h] = s_new
        return carry

    lax.fori_loop(0, bb, body, 0)


def rwkv_step(r, k, v, kk, bv, lw, g, rk, gnw, gnb, s_all, layer, bb):
    b = r.shape[0]
    h, n = RWKV_HEADS, RWKV_HEAD_DIM
    vec = pl.BlockSpec((bb, h, n), lambda i: (i, 0, 0))
    par = pl.BlockSpec((h, n), lambda i: (0, 0))
    st = pl.BlockSpec((bb, h, n, n), lambda i: (i, 0, 0, 0))
    return pl.pallas_call(
        functools.partial(_rwkv_step_kernel, bb=bb),
        grid=(b // bb,),
        in_specs=[vec] * 7 + [par] * 3 + [pl.BlockSpec((None, bb, h, n, n), lambda i: (layer, i, 0, 0, 0))],
        out_specs=[vec, st],
        out_shape=[jax.ShapeDtypeStruct((b, h, n), F32), jax.ShapeDtypeStruct((b, h, n, n), F32)],
        compiler_params=_cp(("parallel",)),
        name="rwkv_step",
    )(r, k, v, kk, bv, lw, g, rk, gnw, gnb, s_all)


def _mla_prompt_kernel(q_ref, k_ref, o_ref, m_scr, l_scr, acc_scr, *, t):
    i = pl.program_id(1)
    nh = MLA_HEADS
    q = jnp.concatenate([q_ref[0, :, h * QCAT:(h + 1) * QCAT] for h in range(nh)], axis=0)
    row = lax.broadcasted_iota(jnp.int32, (t, 1), 0)
    causal = lax.broadcasted_iota(jnp.int32, (1, t), 1) <= jnp.concatenate([row] * nh, axis=0)
    ones = jnp.ones((t, LANES), BF16)
    m_scr[...] = jnp.full_like(m_scr, NEG)
    l_scr[...] = jnp.zeros_like(l_scr)
    acc_scr[...] = jnp.zeros_like(acc_scr)

    def step(j, masked):
        kb = k_ref[0, pl.ds(pl.multiple_of(j * t, t), t), :]
        s = _dot_nt(q, kb)
        if masked:
            s = jnp.where(causal, s, NEG)
        m_old = m_scr[...]
        m_new = jnp.maximum(m_old, jnp.max(s, axis=1, keepdims=True))
        p = jnp.exp(s - jnp.concatenate([m_new] * (t // LANES), axis=1))
        alpha = jnp.exp(m_old - m_new)
        pv = _dot(_bf(p), jnp.concatenate([kb[:, :KV_LORA], ones], axis=1))
        acc_scr[...] = alpha * acc_scr[...] + pv[:, :KV_LORA]
        l_scr[...] = alpha * l_scr[...] + pv[:, KV_LORA:]
        m_scr[...] = m_new

    def body(j, carry):
        step(j, False)
        return carry

    lax.fori_loop(0, i, body, 0)
    step(i, True)
    o = acc_scr[...] / l_scr[...]
    for h in range(nh):
        o_ref[0, :, h * KV_LORA:(h + 1) * KV_LORA] = o[h * t:(h + 1) * t].astype(o_ref.dtype)


def mla_prompt(q_cat, k_cat, tq):
    b, t, _ = q_cat.shape
    rows = MLA_HEADS * tq
    return pl.pallas_call(
        functools.partial(_mla_prompt_kernel, t=tq),
        grid=(b, t // tq),
        in_specs=[pl.BlockSpec((1, tq, MLA_HEADS * QCAT), lambda i, j: (i, j, 0)),
                  pl.BlockSpec((1, t, QCAT), lambda i, j: (i, 0, 0))],
        out_specs=pl.BlockSpec((1, tq, MLA_HEADS * KV_LORA), lambda i, j: (i, j, 0)),
        out_shape=jax.ShapeDtypeStruct((b, t, MLA_HEADS * KV_LORA), BF16),
        scratch_shapes=[pltpu.VMEM((rows, LANES), F32), pltpu.VMEM((rows, LANES), F32), pltpu.VMEM((rows, KV_LORA), F32)],
        compiler_params=_cp(("parallel", "arbitrary")),
        name="mla_prompt",
    )(q_cat, k_cat)


def _mla_sample_kernel(pt_ref, q_ref, knew_ref, ckv_hbm, krt_hbm, o_ref, ckv_buf, kr_buf, s_scr, sem,
                       *, layer, n_pages, n_seq):
    b = pl.program_id(0)
    slot = b % 2

    def start_page(seq, sl, i):
        page = pt_ref[seq * n_pages + i]
        pltpu.make_async_copy(ckv_hbm.at[layer, page], ckv_buf.at[sl, i], sem.at[sl, 0]).start()
        pltpu.make_async_copy(krt_hbm.at[layer, page], kr_buf.at[sl, i], sem.at[sl, 1]).start()

    @pl.when(b == 0)
    def _():
        def body(i, carry):
            start_page(0, 0, i)
            return carry
        lax.fori_loop(0, n_pages, body, 0)

    pltpu.make_async_copy(ckv_hbm.at[layer, pl.ds(0, n_pages)], ckv_buf.at[slot], sem.at[slot, 0]).wait()
    pltpu.make_async_copy(krt_hbm.at[layer, pl.ds(0, n_pages)], kr_buf.at[slot], sem.at[slot, 1]).wait()

    q = q_ref[0]
    ql = q[:, :KV_LORA]
    qr = q[:, KV_LORA:KV_LORA + QK_ROPE_DIM]

    def score_pages(prefetch):
        for pi in range(n_pages):
            if prefetch:
                for pj in range(2 * pi, min(2 * pi + 2, n_pages)):
                    start_page(b + 1, 1 - slot, pj)
            s_scr[pi] = _dot_nt(ql, _bf(ckv_buf[slot, pi])) + _dot(qr, _bf(kr_buf[slot, pi]))

    @pl.when(b + 1 < n_seq)
    def _():
        score_pages(True)

    @pl.when(b + 1 == n_seq)
    def _():
        score_pages(False)

    k_new = knew_ref[0].astype(F32)
    s_new = jnp.sum(q.astype(F32) * k_new, axis=1, keepdims=True)
    s = s_scr[...]
    m = jnp.maximum(jnp.max(jnp.max(s, axis=0), axis=1, keepdims=True), s_new)
    p = jnp.exp(s - m)
    p_new = jnp.exp(s_new - m)
    l = jnp.sum(jnp.sum(p, axis=0), axis=1, keepdims=True) + p_new
    s_scr[...] = p

    acc = p_new * k_new[:, :KV_LORA]
    for pi in range(n_pages):
        acc = acc + _dot(_bf(s_scr[pi]), _bf(ckv_buf[slot, pi]))
    o_ref[0] = (acc / l).astype(o_ref.dtype)


def mla_sample(page_table_flat, q, k_new, cache_ckv, cache_krope_t, layer, n_pages):
    n_seq = q.shape[0]
    grid_spec = pltpu.PrefetchScalarGridSpec(
        num_scalar_prefetch=1,
        grid=(n_seq,),
        in_specs=[pl.BlockSpec((1, MLA_HEADS, QCAT), lambda bi, pt: (bi, 0, 0)),
                  pl.BlockSpec((1, 1, QCAT), lambda bi, pt: (bi, 0, 0)),
                  pl.BlockSpec(memory_space=pl.ANY),
                  pl.BlockSpec(memory_space=pl.ANY)],
        out_specs=pl.BlockSpec((1, MLA_HEADS, KV_LORA), lambda bi, pt: (bi, 0, 0)),
        scratch_shapes=[pltpu.VMEM((2, n_pages, PAGE_SIZE, KV_LORA), F32),
                        pltpu.VMEM((2, n_pages, QK_ROPE_DIM, PAGE_SIZE), F32),
                        pltpu.VMEM((n_pages, MLA_HEADS, PAGE_SIZE), F32),
                        pltpu.SemaphoreType.DMA((2, 2))],
    )
    return pl.pallas_call(
        functools.partial(_mla_sample_kernel, layer=layer, n_pages=n_pages, n_seq=n_seq),
        grid_spec=grid_spec,
        out_shape=jax.ShapeDtypeStruct((n_seq, MLA_HEADS, KV_LORA), BF16),
        compiler_params=_cp(("arbitrary",)),
        name="mla_sample",
    )(page_table_flat, q, k_new, cache_ckv, cache_krope_t)


def _cross_sample_kernel(q_ref, mk_ref, mv_ref, o_ref):
    scale = MEM_HEAD_DIM ** -0.5
    q = q_ref[0].astype(F32)
    q4 = jnp.concatenate([q[:, h * MEM_HEAD_DIM:(h + 1) * MEM_HEAD_DIM] for h in range(MEM_HEADS)], axis=0)
    s = jnp.sum(mk_ref[0, 0] * q4[None], axis=2, keepdims=True) * scale
    m = jnp.max(s, axis=0, keepdims=True)
    p = jnp.exp(s - m)
    l = jnp.sum(p, axis=0)
    o = jnp.sum(p * mv_ref[0, 0], axis=0) / l
    for h in range(MEM_HEADS):
        o_ref[0, :, h * MEM_HEAD_DIM:(h + 1) * MEM_HEAD_DIM] = o[h:h + 1].astype(o_ref.dtype)


def cross_sample(q, mem_k, mem_v, layer):
    b, _, d = q.shape
    blk = pl.BlockSpec((1, 1, N_MEM, MEM_HEADS, MEM_HEAD_DIM), lambda i: (layer, i, 0, 0, 0))
    return pl.pallas_call(
        _cross_sample_kernel,
        grid=(b,),
        in_specs=[pl.BlockSpec((1, 1, d), lambda i: (i, 0, 0)), blk, blk],
        out_specs=pl.BlockSpec((1, 1, d), lambda i: (i, 0, 0)),
        out_shape=jax.ShapeDtypeStruct((b, 1, d), BF16),
        compiler_params=_cp(("parallel",)),
        name="cross_sample",
    )(q, mem_k, mem_v)


def _rot_cols(w):
    half = QK_ROPE_DIM // 2
    return jnp.concatenate([-w[..., half:], w[..., :half]], axis=-1)


def _pad_lanes(w):
    return jnp.pad(w, [(0, 0)] * (w.ndim - 1) + [(0, LANES - w.shape[-1])])


def _layer_weights(l, w_in, shift_mu, rwkv_w0, rwkv_w_up, rwkv_a0, rwkv_a_up, rwkv_g_up, rwkv_k_k, rwkv_k_a,
                   rwkv_r_k, rwkv_gn_w, rwkv_gn_b, mla_q_norm, mla_w_uq, mla_kv_norm, mla_w_uk, mla_w_uv):
    w3 = RWKV_WIDTH
    o_kr = RWKV_PROJ + Q_LORA + KV_LORA
    w_kr = w_in[l][:, o_kr:]
    w_in_ext = _bf(jnp.concatenate([w_in[l][:, :o_kr], _pad_lanes(w_kr), _pad_lanes(_rot_cols(w_kr))], axis=1))
    w_lora = jnp.zeros((LORA_SLAB, 3 * w3), F32)
    w_lora = w_lora.at[:DECAY_LORA, :w3].set(rwkv_w_up[l])
    w_lora = w_lora.at[DECAY_LORA:DECAY_LORA + ICLR_LORA, w3:2 * w3].set(rwkv_a_up[l])
    w_lora = w_lora.at[DECAY_LORA + ICLR_LORA:, 2 * w3:].set(rwkv_g_up[l])
    uq = mla_w_uq[l].reshape(Q_LORA, MLA_HEADS, QK_NOPE_DIM + QK_ROPE_DIM)
    uq_nope = uq[:, :, :QK_NOPE_DIM].reshape(Q_LORA, MLA_HEADS * QK_NOPE_DIM)
    uq_rope = uq[:, :, QK_NOPE_DIM:]
    w_q = _bf(jnp.concatenate([uq_nope,
                               _pad_lanes(uq_rope).reshape(Q_LORA, MLA_HEADS * LANES),
                               _pad_lanes(_rot_cols(uq_rope)).reshape(Q_LORA, MLA_HEADS * LANES)], axis=1))
    eye_h = jnp.eye(MLA_HEADS, dtype=F32)
    wuk_bd = _bf(jnp.einsum('chd,hg->hdgc', mla_w_uk[l], eye_h).reshape(MLA_HEADS * QK_NOPE_DIM, MLA_HEADS * KV_LORA))
    wuv_bd = _bf(jnp.einsum('chd,hg->hcgd', mla_w_uv[l], eye_h).reshape(MLA_HEADS * KV_LORA, MLA_HEADS * V_HEAD_DIM))
    row = lambda a: a.reshape(1, -1)
    return dict(
        w_in_ext=w_in_ext, mu=row(shift_mu[l]), w0=row(rwkv_w0[l]), a0=row(rwkv_a0[l]), w_lora=_bf(w_lora),
        k_k=row(rwkv_k_k[l]), k_a=row(rwkv_k_a[l]), rk=row(rwkv_r_k[l]), gnw=row(rwkv_gn_w[l]), gnb=row(rwkv_gn_b[l]),
        q_norm=row(mla_q_norm[l]), kv_norm=row(mla_kv_norm[l]), w_q=w_q, wuk_bd=wuk_bd, wuv_bd=wuv_bd)


def _rope_tables(pos):
    half = QK_ROPE_DIM // 2
    inv_freq = ROPE_THETA ** (-jnp.arange(half, dtype=F32) / half)
    ang = pos.astype(F32)[:, None] * inv_freq[None, :]
    cos, sin = jnp.cos(ang), jnp.sin(ang)
    return (_pad_lanes(jnp.concatenate([cos, cos], axis=1)), _pad_lanes(jnp.concatenate([sin, sin], axis=1)))


def kernel(x_prompt, x_sample, cache_ckv, cache_krope, state_wkv, state_shift, cache_mem_k, cache_mem_v, page_table, mem_prompt, norm_ffn1, ffn1_w_gate, ffn1_w_up, ffn1_w_down, norm_mix, w_in, shift_mu, rwkv_w0, rwkv_w_up, rwkv_a0, rwkv_a_up, rwkv_g_up, rwkv_k_k, rwkv_k_a, rwkv_r_k, rwkv_gn_w, rwkv_gn_b, mla_q_norm, mla_w_uq, mla_kv_norm, mla_w_uk, mla_w_uv, w_out, norm_cross, norm_mem, mem_w_q, mem_w_k, mem_w_v, mem_w_o, norm_ffn2, ffn2_w_gate, ffn2_w_up, ffn2_w_down, norm_final):
    bp, tp, d = x_prompt.shape
    bs, ts, _ = x_sample.shape
    assert ts == 1
    n_pages = page_table.shape[1]
    past_len = n_pages * PAGE_SIZE
    hh, nn = RWKV_HEADS, RWKV_HEAD_DIM
    row = lambda a: a.reshape(1, -1)

    head_of = jnp.arange(RWKV_WIDTH) // nn
    ones_bd = (head_of[:, None] == head_of[None, :]).astype(BF16)
    cos_p, sin_p = _rope_tables(jnp.arange(tp))
    cos_s, sin_s = _rope_tables(jnp.full((bs,), past_len, jnp.int32))
    pt_flat = page_table.reshape(-1)
    krope_t = jnp.swapaxes(cache_krope, 2, 3)

    lw = [_layer_weights(l, w_in, shift_mu, rwkv_w0, rwkv_w_up, rwkv_a0, rwkv_a_up, rwkv_g_up, rwkv_k_k, rwkv_k_a,
                         rwkv_r_k, rwkv_gn_w, rwkv_gn_b, mla_q_norm, mla_w_uq, mla_kv_norm, mla_w_uk, mla_w_uv)
          for l in range(DEPTH)]

    tm_p, tm_s = 512, bs

    def run_ffn(x2, l, which, tm, final):
        if which == 1:
            g, wg, wu, wd = norm_ffn1[l], ffn1_w_gate, ffn1_w_up, ffn1_w_down
        else:
            g, wg, wu, wd = norm_ffn2[l], ffn2_w_gate, ffn2_w_up, ffn2_w_down
        return ffn(x2, row(g), wg, wu, wd, l, row(norm_final), final, tm)

    mem_k_all, mem_v_all = memory_kv(mem_prompt.reshape(bp * N_MEM, d), norm_mem.reshape(DEPTH, 1, d), mem_w_k, mem_w_v, 512)
    mem_k_p = mem_k_all.reshape(DEPTH, bp, N_MEM, d)
    mem_v_p = mem_v_all.reshape(DEPTH, bp, N_MEM, d)

    def trunk(x, sample):
        b, t, _ = x.shape
        m = b * t
        tm = tm_s if sample else tm_p
        x2 = x.reshape(m, d)
        ckvs, krs, wkvs, shifts = [], [], [], []
        for l in range(DEPTH):
            p = lw[l]
            x2 = run_ffn(x2, l, 1, tm, False)
            if sample:
                outs = prep(x2.reshape(1, m, d), row(norm_mix[l]), p['w_in_ext'],
                            state_shift[l].reshape(1, m, RWKV_PROJ), True,
                            p['mu'], p['w0'], p['a0'], p['w_lora'], p['k_k'], p['k_a'], ones_bd, p['q_norm'],
                            p['kv_norm'], p['w_q'], p['wuk_bd'], cos_s, sin_s, m)
            else:
                outs = prep(x2.reshape(b, t, d), row(norm_mix[l]), p['w_in_ext'],
                            jnp.zeros((b, 1, RWKV_PROJ), F32), False,
                            p['mu'], p['w0'], p['a0'], p['w_lora'], p['k_k'], p['k_a'], ones_bd, p['q_norm'],
                            p['kv_norm'], p['w_q'], p['wuk_bd'], cos_p, sin_p, 256)
            shift_last = outs[11].reshape(b, RWKV_PROJ)
            r, k, v, kk, bv, lwd, g, ckv, kr, q_cat, k_cat = outs[:11]
            if sample:
                to_heads = lambda a: a.reshape(m, hh, nn)
                o_rwkv, wkv = rwkv_step(to_heads(r), to_heads(k), to_heads(v), to_heads(kk), to_heads(bv),
                                        to_heads(lwd), to_heads(g), p['rk'].reshape(hh, nn), p['gnw'].reshape(hh, nn),
                                        p['gnb'].reshape(hh, nn), state_wkv, l, 8)
                o_rwkv = _bf(o_rwkv.reshape(m, RWKV_WIDTH))
                o_lat = mla_sample(pt_flat, q_cat.reshape(m, MLA_HEADS, QCAT), k_cat.reshape(m, 1, QCAT),
                                   cache_ckv, krope_t, l, n_pages).reshape(m, MLA_HEADS * KV_LORA)
                x2 = mix_out(o_rwkv, o_lat, p['wuv_bd'], _bf(w_out[l]), x2, tm)
                q = norm_matmul(x2, row(norm_cross[l]), _bf(mem_w_q[l]), F32, tm)
                att = cross_sample(q.reshape(b, t, d), cache_mem_k, cache_mem_v, l)
                x2 = matmul_res(att.reshape(m, d), _bf(mem_w_o[l]), x2, tm)
            else:
                o_rwkv, s_bd = rwkv_chunked(r, k, v, kk, bv, lwd, g, p['rk'], p['gnw'], p['gnb'], CHUNK, CHUNK_SEQS)
                wkv = jnp.stack([s_bd[:, :, :nn, :nn], s_bd[:, :, nn:, nn:]], axis=2).reshape(b, hh, nn, nn)
                o_lat = mla_prompt(q_cat, k_cat, 256)
                x2 = post_mix(o_rwkv, o_lat, x2.reshape(b, t, d), p['wuv_bd'], _bf(w_out[l]), row(norm_cross[l]),
                              _bf(mem_w_q[l]), mem_k_p, mem_v_p, l, _bf(mem_w_o[l]), 512).reshape(m, d)
            x2 = run_ffn(x2, l, 2, tm, l == DEPTH - 1)
            ckvs.append(ckv.reshape(b, t, KV_LORA))
            krs.append(kr.reshape(b, t, QK_ROPE_DIM))
            wkvs.append(wkv)
            shifts.append(shift_last)
        return x2.reshape(b, t, d), jnp.stack(ckvs), jnp.stack(krs), jnp.stack(wkvs), jnp.stack(shifts)

    y_p, ckv_p, kr_p, wkv_p, shift_p = trunk(x_prompt, False)
    y_s, ckv_s, kr_s, wkv_s, shift_s = trunk(x_sample, True)
    mem_shape = (DEPTH, bp, N_MEM, MEM_HEADS, MEM_HEAD_DIM)
    return (y_p, y_s, ckv_p, kr_p, wkv_p, shift_p,
            mem_k_all.reshape(mem_shape), mem_v_all.reshape(mem_shape),
            ckv_s, kr_s, wkv_s, shift_s)
```
